```python
import math
import jax, jax.numpy as jnp
from jax import lax
import numpy as np

D_MODEL = 1024
BATCH = 2
SEQ = 8192
DEPTH = 4

N_A_LAYERS = DEPTH // 2
N_B_LAYERS = DEPTH - N_A_LAYERS
CONV_WIDTH = 31
N_HEADS = 16
HEAD_DIM = D_MODEL // N_HEADS
D_FF = 4 * D_MODEL
D_PLE = 256
Q_BLOCK = 128
EPS = 1e-6
NEG_BIG = -1e30

kernel_name = "yoco_conformer_fox_hybrid"


def rmsnorm(x, g):
    xf = x.astype(jnp.float32)
    y = xf * lax.rsqrt(jnp.mean(xf * xf, axis=-1, keepdims=True) + EPS)
    return (y * g.astype(jnp.float32)).astype(x.dtype)


def layernorm(x, g, b):
    xf = x.astype(jnp.float32)
    mu = jnp.mean(xf, axis=-1, keepdims=True)
    xc = xf - mu
    var = jnp.mean(xc * xc, axis=-1, keepdims=True)
    y = xc * lax.rsqrt(var + EPS)
    return (y * g.astype(jnp.float32) + b.astype(jnp.float32)).astype(x.dtype)


def conformer_conv(hn, w_pw1, b_pw1, w_dw, b_dw, ln_g, ln_b, w_pw2, b_pw2):
    u = hn @ w_pw1 + b_pw1
    a, g = jnp.split(u, 2, axis=-1)
    u = a * jax.nn.sigmoid(g)
    kern = w_dw[:, None, :].astype(u.dtype)
    u = lax.conv_general_dilated(
        u, kern, window_strides=(1,), padding=((CONV_WIDTH - 1, 0),),
        dimension_numbers=("NWC", "WIO", "NWC"),
        feature_group_count=D_MODEL) + b_dw
    u = layernorm(u, ln_g, ln_b)
    u = jax.nn.silu(u)
    return u @ w_pw2 + b_pw2


def shared_kv(h, kv_norm, w_kvf, b_f):
    B, S, _ = h.shape
    u = rmsnorm(h, kv_norm) @ w_kvf
    k = u[..., :D_MODEL].reshape(B, S, N_HEADS, HEAD_DIM)
    v = u[..., D_MODEL:2 * D_MODEL].reshape(B, S, N_HEADS, HEAD_DIM)
    f_logit = (u[..., 2 * D_MODEL:] + b_f).astype(jnp.float32)
    log_f = jax.nn.log_sigmoid(f_logit)
    c = jnp.cumsum(log_f, axis=1)
    return k, v, jnp.transpose(c, (0, 2, 1))


def fox_attention(hn, w_q, w_o, k, v, c_bhs):
    B, S, _ = hn.shape
    nb = S // Q_BLOCK
    q = (hn @ w_q).reshape(B, S, N_HEADS, HEAD_DIM) * (HEAD_DIM ** -0.5)
    qb = jnp.transpose(q.reshape(B, nb, Q_BLOCK, N_HEADS, HEAD_DIM), (1, 0, 2, 3, 4))
    cb = jnp.transpose(c_bhs.reshape(B, N_HEADS, nb, Q_BLOCK), (2, 0, 1, 3))
    k_pos = jnp.arange(S)

    def one_block(args):
        q_blk, c_blk, i = args
        s = jnp.einsum("bqhd,bkhd->bhqk", q_blk, k,
                       preferred_element_type=jnp.float32)
        bias = c_blk[:, :, :, None] - c_bhs[:, :, None, :]
        q_pos = i * Q_BLOCK + jnp.arange(Q_BLOCK)
        causal = k_pos[None, :] <= q_pos[:, None]
        s = jnp.where(causal, s + bias, NEG_BIG)
        pr = jax.nn.softmax(s, axis=-1)
        return jnp.einsum("bhqk,bkhd->bqhd", pr.astype(v.dtype), v)

    o = lax.map(one_block, (qb, cb, jnp.arange(nb)))
    o = jnp.transpose(o, (1, 0, 2, 3, 4)).reshape(B, S, D_MODEL)
    return o @ w_o


def setup_inputs(seed: int = 0) -> dict:
    key = jax.random.key(seed)
    ks = jax.random.split(key, 32)
    f32 = jnp.float32
    nrm = lambda k, shape, scale: (jax.random.normal(k, shape, f32) * scale)
    gain = lambda k, shape: 1.0 + 0.05 * jax.random.normal(k, shape, f32)
    D = D_MODEL
    x = jax.random.normal(ks[0], (BATCH, SEQ, D), f32)
    p = jax.random.normal(ks[1], (DEPTH, BATCH, SEQ, D_PLE), f32)
    mix_norm = gain(ks[2], (DEPTH, D))
    conv_w_pw1 = nrm(ks[3], (N_A_LAYERS, D, 2 * D), D ** -0.5)
    conv_b_pw1 = nrm(ks[4], (N_A_LAYERS, 2 * D), 0.02)
    conv_w_dw = nrm(ks[5], (N_A_LAYERS, CONV_WIDTH, D), CONV_WIDTH ** -0.5)
    conv_b_dw = nrm(ks[6], (N_A_LAYERS, D), 0.02)
    conv_ln_g = gain(ks[7], (N_A_LAYERS, D))
    conv_ln_b = nrm(ks[8], (N_A_LAYERS, D), 0.02)
    conv_w_pw2 = nrm(ks[9], (N_A_LAYERS, D, D), 0.5 * D ** -0.5)
    conv_b_pw2 = nrm(ks[10], (N_A_LAYERS, D), 0.02)
    kv_norm = gain(ks[11], (D,))
    w_kvf = jnp.concatenate([
        nrm(ks[12], (D, 2 * D), D ** -0.5),
        nrm(ks[13], (D, N_HEADS), 0.1 * D ** -0.5),
    ], axis=1)
    b_f = jax.random.uniform(ks[14], (N_HEADS,), f32, 1.0, 6.0)
    attn_w_q = nrm(ks[15], (N_B_LAYERS, D, D), D ** -0.5)
    attn_w_o = nrm(ks[16], (N_B_LAYERS, D, D), 0.5 * D ** -0.5)
    ffn_norm = gain(ks[17], (DEPTH, D))
    ffn_w1 = nrm(ks[18], (DEPTH, D, D_FF), D ** -0.5)
    ffn_w2 = nrm(ks[19], (DEPTH, D_FF, D), 0.5 * D_FF ** -0.5)
    ple_norm = gain(ks[20], (DEPTH, D))
    ple_w_gate = nrm(ks[21], (DEPTH, D, D), D ** -0.5)
    ple_w_proj = nrm(ks[22], (DEPTH, D_PLE, D), 0.5 * D_PLE ** -0.5)
    final_norm = gain(ks[23], (D,))
    return {"x": x, "p": p, "mix_norm": mix_norm,
            "conv_w_pw1": conv_w_pw1, "conv_b_pw1": conv_b_pw1,
            "conv_w_dw": conv_w_dw, "conv_b_dw": conv_b_dw,
            "conv_ln_g": conv_ln_g, "conv_ln_b": conv_ln_b,
            "conv_w_pw2": conv_w_pw2, "conv_b_pw2": conv_b_pw2,
            "kv_norm": kv_norm, "w_kvf": w_kvf, "b_f": b_f,
            "attn_w_q": attn_w_q, "attn_w_o": attn_w_o,
            "ffn_norm": ffn_norm, "ffn_w1": ffn_w1, "ffn_w2": ffn_w2,
            "ple_norm": ple_norm, "ple_w_gate": ple_w_gate, "ple_w_proj": ple_w_proj,
            "final_norm": final_norm}


def reference(x, p, mix_norm, conv_w_pw1, conv_b_pw1, conv_w_dw, conv_b_dw,
              conv_ln_g, conv_ln_b, conv_w_pw2, conv_b_pw2, kv_norm, w_kvf, b_f,
              attn_w_q, attn_w_o, ffn_norm, ffn_w1, ffn_w2, ple_norm, ple_w_gate,
              ple_w_proj, final_norm):
    h = x
    k = v = c_bhs = None
    for i in range(DEPTH):
        hn = rmsnorm(h, mix_norm[i])
        if i < N_A_LAYERS:
            h = h + conformer_conv(hn, conv_w_pw1[i], conv_b_pw1[i], conv_w_dw[i],
                                   conv_b_dw[i], conv_ln_g[i], conv_ln_b[i],
                                   conv_w_pw2[i], conv_b_pw2[i])
        else:
            j = i - N_A_LAYERS
            if j == 0:
                k, v, c_bhs = shared_kv(h, kv_norm, w_kvf, b_f)
            h = h + fox_attention(hn, attn_w_q[j], attn_w_o[j], k, v, c_bhs)
        hn = rmsnorm(h, ffn_norm[i])
        h = h + jnp.square(jax.nn.relu(hn @ ffn_w1[i])) @ ffn_w2[i]
        gate = jax.nn.sigmoid(rmsnorm(h, ple_norm[i]) @ ple_w_gate[i])
        h = h + gate * (p[i] @ ple_w_proj[i])
    return rmsnorm(h, final_norm)
```

```python
import functools
import math

import jax
import jax.numpy as jnp
from jax import lax
from jax.experimental import pallas as pl
from jax.experimental.pallas import tpu as pltpu

F32 = jnp.float32
BF16 = jnp.bfloat16

EPS = 1e-6
NEG_BIG = -1e30
N_HEADS = 16
CONV_WIDTH = 31
LOG2E = math.log2(math.e)

LANES = 128
HALO = 32
TM = 512
TM_CONV = 256
TQ = 512
VMEM_LIMIT = 56 * 1024 * 1024


def _params(*sem):
    return pltpu.CompilerParams(dimension_semantics=sem, vmem_limit_bytes=VMEM_LIMIT)


def _resident(shape):
    nd = len(shape)
    return pl.BlockSpec(shape, lambda *_: (0,) * nd)


def _rms(x, g):
    return x * lax.rsqrt(jnp.mean(x * x, axis=-1, keepdims=True) + EPS) * g


def _split3(x):
    hi = x.astype(BF16)
    r = x - hi.astype(F32)
    mid = r.astype(BF16)
    lo = (r - mid.astype(F32)).astype(BF16)
    return hi, mid, lo


def _pw1_glu_kernel(h_ref, g_ref, wa_ref, wg_ref, ba_ref, bg_ref, o_ref):
    xn = _rms(h_ref[...], g_ref[...]).astype(BF16)
    a = jnp.dot(xn, wa_ref[...], preferred_element_type=F32) + ba_ref[...]
    gt = jnp.dot(xn, wg_ref[...], preferred_element_type=F32) + bg_ref[...]
    o_ref[...] = (a * jax.nn.sigmoid(gt)).astype(o_ref.dtype)


def _pw1_glu(h, g, w, b):
    n, d = h.shape
    wa, wg = w[:, :d].astype(BF16), w[:, d:].astype(BF16)
    ba, bg = b[:d].reshape(1, d), b[d:].reshape(1, d)
    row = pl.BlockSpec((TM, d), lambda i: (i, 0))
    return pl.pallas_call(
        _pw1_glu_kernel,
        grid=(n // TM,),
        in_specs=[row, _resident((1, d)), _resident((d, d)), _resident((d, d)),
                  _resident((1, d)), _resident((1, d))],
        out_specs=row,
        out_shape=jax.ShapeDtypeStruct((n, d), BF16),
        compiler_params=_params("parallel"),
        name="pw1_glu",
    )(h, g.reshape(1, d), wa, wg, ba, bg)


def _conv_kernel(x_ref, halo_ref, h_ref, wdw_ref, bdw_ref, lg_ref, lb_ref, w2_ref, b2_ref,
                 o_ref, xs_ref, y_ref):
    tm, d = x_ref.shape
    first = pl.program_id(1) == 0
    xs_ref[0:HALO, :] = jnp.where(first, 0.0, halo_ref[...].astype(F32))
    xs_ref[HALO:, :] = x_ref[...].astype(F32)

    def strip(c, carry):
        lanes = pl.ds(pl.multiple_of(c * LANES, LANES), LANES)
        w = [jnp.broadcast_to(wdw_ref[k:k + 1, lanes], (8, LANES)) for k in range(CONV_WIDTH)]
        bias = jnp.broadcast_to(bdw_ref[:, lanes], (8, LANES))
        for r in range(tm // 8):
            acc = bias
            for k in range(CONV_WIDTH):
                r0 = r * 8 + HALO - (CONV_WIDTH - 1) + k
                acc = acc + w[k] * xs_ref[r0:r0 + 8, lanes]
            y_ref[r * 8:(r + 1) * 8, lanes] = acc
        return carry

    lax.fori_loop(0, d // LANES, strip, 0)

    y = y_ref[...]
    mu = jnp.mean(y, axis=-1, keepdims=True)
    yc = y - mu
    var = jnp.mean(yc * yc, axis=-1, keepdims=True)
    z = yc * lax.rsqrt(var + EPS) * lg_ref[...] + lb_ref[...]
    z = z * jax.nn.sigmoid(z)
    out = jnp.dot(z.astype(BF16), w2_ref[...], preferred_element_type=F32) + b2_ref[...]
    o_ref[...] = h_ref[...] + out


def _conv_block(glu, h, w_dw, b_dw, ln_g, ln_b, w_pw2, b_pw2):
    b, s, d = glu.shape
    tm = TM_CONV
    per = tm // HALO
    wdw = jnp.zeros((HALO, d), F32).at[:CONV_WIDTH].set(w_dw)
    row = pl.BlockSpec((None, tm, d), lambda bi, i: (bi, i, 0))
    halo = pl.BlockSpec((None, HALO, d), lambda bi, i: (bi, jnp.maximum(i * per - 1, 0), 0))
    vec = _resident((1, d))
    return pl.pallas_call(
        _conv_kernel,
        grid=(b, s // tm),
        in_specs=[row, halo, row, _resident((HALO, d)), vec, vec, vec, _resident((d, d)), vec],
        out_specs=row,
        out_shape=jax.ShapeDtypeStruct((b, s, d), F32),
        scratch_shapes=[pltpu.VMEM((tm + HALO, d), F32), pltpu.VMEM((tm, d), F32)],
        compiler_params=_params("parallel", "parallel"),
        name="dwconv_ln_pw2",
    )(glu, glu, h, wdw, b_dw.reshape(1, d), ln_g.reshape(1, d), ln_b.reshape(1, d),
      w_pw2.astype(BF16), b_pw2.reshape(1, d))


def _ffn_kernel(h_ref, g_ref, w1_ref, w2_ref, o_ref, *, chunk):
    h = h_ref[...]
    xn = _rms(h, g_ref[...]).astype(BF16)
    acc = h
    for c in range(w1_ref.shape[1] // chunk):
        a = jnp.dot(xn, w1_ref[:, c * chunk:(c + 1) * chunk], preferred_element_type=F32)
        a = jnp.square(jnp.maximum(a, 0.0)).astype(BF16)
        acc = acc + jnp.dot(a, w2_ref[c * chunk:(c + 1) * chunk, :], preferred_element_type=F32)
    o_ref[...] = acc


def _ffn(h, g, w1, w2):
    n, d = h.shape
    dff = w1.shape[1]
    row = pl.BlockSpec((TM, d), lambda i: (i, 0))
    return pl.pallas_call(
        functools.partial(_ffn_kernel, chunk=1024),
        grid=(n // TM,),
        in_specs=[row, _resident((1, d)), _resident((d, dff)), _resident((dff, d))],
        out_specs=row,
        out_shape=jax.ShapeDtypeStruct((n, d), F32),
        compiler_params=_params("parallel"),
        name="ffn",
    )(h, g.reshape(1, d), w1.astype(BF16), w2.astype(BF16))


def _ple_kernel(h_ref, p_ref, g_ref, wg_ref, wp_ref, o_ref):
    h = h_ref[...]
    xn = _rms(h, g_ref[...]).astype(BF16)
    gate = jax.nn.sigmoid(jnp.dot(xn, wg_ref[...], preferred_element_type=F32))
    proj = jnp.dot(p_ref[...].astype(BF16), wp_ref[...], preferred_element_type=F32)
    o_ref[...] = h + gate * proj


def _ple(h, p, g, w_gate, w_proj):
    n, d = h.shape
    dp = p.shape[1]
    row = pl.BlockSpec((TM, d), lambda i: (i, 0))
    return pl.pallas_call(
        _ple_kernel,
        grid=(n // TM,),
        in_specs=[row, pl.BlockSpec((TM, dp), lambda i: (i, 0)), _resident((1, d)),
                  _resident((d, d)), _resident((dp, d))],
        out_specs=row,
        out_shape=jax.ShapeDtypeStruct((n, d), F32),
        compiler_params=_params("parallel"),
        name="ple",
    )(h, p, g.reshape(1, d), w_gate.astype(BF16), w_proj.astype(BF16))


def _gate_kernel(h_ref, g_ref, wf_ref, bf_ref, c_ref, carry_ref):
    @pl.when(pl.program_id(1) == 0)
    def _():
        carry_ref[...] = jnp.zeros_like(carry_ref)

    tm = h_ref.shape[0]
    xn = _rms(h_ref[...], g_ref[...]).astype(BF16)
    f = jnp.dot(xn, wf_ref[...], preferred_element_type=F32) + bf_ref[...]
    logf = jnp.minimum(f, 0.0) - jnp.log1p(jnp.exp(-jnp.abs(f)))
    rows = lax.broadcasted_iota(jnp.int32, (tm, tm), 0)
    cols = lax.broadcasted_iota(jnp.int32, (tm, tm), 1)
    tri = jnp.where(rows >= cols, 1.0, 0.0).astype(BF16)
    hi, mid, lo = _split3(logf)
    c = (jnp.dot(tri, hi, preferred_element_type=F32)
         + jnp.dot(tri, mid, preferred_element_type=F32)
         + jnp.dot(tri, lo, preferred_element_type=F32)) + carry_ref[...]
    c_ref[...] = c
    carry_ref[...] = c[tm - 1:tm, :]


def _gates(h, g, w_f, b_f):
    b, s, d = h.shape
    nh = w_f.shape[1]
    wf = jnp.zeros((d, LANES), F32).at[:, :nh].set(w_f).astype(BF16)
    bf = jnp.zeros((1, LANES), F32).at[0, :nh].set(b_f)
    return pl.pallas_call(
        _gate_kernel,
        grid=(b, s // TM),
        in_specs=[pl.BlockSpec((None, TM, d), lambda bi, i: (bi, i, 0)), _resident((1, d)),
                  _resident((d, LANES)), _resident((1, LANES))],
        out_specs=pl.BlockSpec((None, TM, LANES), lambda bi, i: (bi, i, 0)),
        out_shape=jax.ShapeDtypeStruct((b, s, LANES), F32),
        scratch_shapes=[pltpu.VMEM((1, LANES), F32)],
        compiler_params=_params("parallel", "arbitrary"),
        name="forget_gate_cumsum",
    )(h, g.reshape(1, d), wf, bf)


def _place(c2, e_ref):
    hi, mid, lo = _split3(c2)
    return (jnp.dot(hi, e_ref[0], preferred_element_type=F32)
            + jnp.dot(mid, e_ref[1], preferred_element_type=F32)
            + jnp.dot(lo, e_ref[2], preferred_element_type=F32))


def _kv_kernel(h_ref, c_ref, g_ref, wk_ref, wv_ref, e_ref, ones_ref, k_ref, v_ref):
    xn = _rms(h_ref[...], g_ref[...]).astype(BF16)
    k = jnp.dot(xn, wk_ref[...], preferred_element_type=F32)
    k = k + _place(-(c_ref[...] * LOG2E), e_ref) + ones_ref[...]
    k_ref[...] = k.astype(k_ref.dtype)
    v_ref[...] = jnp.dot(xn, wv_ref[...], preferred_element_type=F32).astype(v_ref.dtype)


def _q_kernel(h_ref, c_ref, g_ref, wq_ref, e_ref, ones_ref, q_ref, *, scale):
    xn = _rms(h_ref[...], g_ref[...]).astype(BF16)
    q = jnp.dot(xn, wq_ref[...], preferred_element_type=F32) * scale
    q = q + _place(c_ref[...] * LOG2E, e_ref) + ones_ref[...]
    q_ref[...] = q.astype(q_ref.dtype)


def _widen_cols(w, nh):
    d, dm = w.shape
    dh = dm // nh
    w = w.reshape(d, nh, dh)
    return jnp.concatenate([w, jnp.zeros((d, nh, LANES - dh), w.dtype)], axis=2).reshape(d, nh * LANES)


def _aug_consts(nh, dh, c_off, one_off):
    e = jnp.zeros((3, LANES, nh * LANES), F32)
    ones = jnp.zeros((1, nh * LANES), F32)
    heads = jnp.arange(nh)
    for r in range(3):
        e = e.at[r, heads, heads * LANES + dh + c_off + r].set(1.0)
        ones = ones.at[0, heads * LANES + dh + one_off + r].set(1.0)
    return e.astype(BF16), ones


def _kv(h, c, g, w_k, w_v):
    n, d = h.shape
    nh = N_HEADS
    dh = d // nh
    e, ones = _aug_consts(nh, dh, 3, 0)
    row = pl.BlockSpec((TM, d), lambda i: (i, 0))
    wide = pl.BlockSpec((TM, nh * LANES), lambda i: (i, 0))
    return pl.pallas_call(
        _kv_kernel,
        grid=(n // TM,),
        in_specs=[row, pl.BlockSpec((TM, LANES), lambda i: (i, 0)), _resident((1, d)),
                  _resident((d, nh * LANES)), _resident((d, d)), _resident((3, LANES, nh * LANES)),
                  _resident((1, nh * LANES))],
        out_specs=[wide, row],
        out_shape=[jax.ShapeDtypeStruct((n, nh * LANES), BF16), jax.ShapeDtypeStruct((n, d), BF16)],
        compiler_params=_params("parallel"),
        name="kv_proj",
    )(h, c, g.reshape(1, d), _widen_cols(w_k, nh).astype(BF16), w_v.astype(BF16), e, ones)


def _q(h, c, g, w_q):
    n, d = h.shape
    nh = N_HEADS
    dh = d // nh
    e, ones = _aug_consts(nh, dh, 0, 3)
    row = pl.BlockSpec((TM, d), lambda i: (i, 0))
    wide = pl.BlockSpec((TM, nh * LANES), lambda i: (i, 0))
    return pl.pallas_call(
        functools.partial(_q_kernel, scale=dh ** -0.5 * LOG2E),
        grid=(n // TM,),
        in_specs=[row, pl.BlockSpec((TM, LANES), lambda i: (i, 0)), _resident((1, d)),
                  _resident((d, nh * LANES)), _resident((3, LANES, nh * LANES)),
                  _resident((1, nh * LANES))],
        out_specs=wide,
        out_shape=jax.ShapeDtypeStruct((n, nh * LANES), BF16),
        compiler_params=_params("parallel"),
        name="q_proj",
    )(h, c, g.reshape(1, d), _widen_cols(w_q, nh).astype(BF16), e, ones)


def _attn_kernel(q_ref, k_ref, v_ref, o_ref, *, dh):
    tq = q_ref.shape[0]
    i = pl.program_id(2)
    rows = lax.broadcasted_iota(jnp.int32, (tq, tq), 0)
    cols = lax.broadcasted_iota(jnp.int32, (tq, tq), 1)
    causal = rows >= cols
    outs = []
    for hh in range(2):
        hl = slice(hh * LANES, (hh + 1) * LANES)
        q = q_ref[:, hl]

        def step(j, carry, masked):
            m, l, acc = carry
            kv_rows = pl.ds(pl.multiple_of(j * tq, tq), tq)
            k = k_ref[kv_rows, hl]
            v = v_ref[kv_rows, :]
            s = lax.dot_general(q, k, (((1,), (1,)), ((), ())), preferred_element_type=F32)
            if masked:
                s = jnp.where(causal, s, NEG_BIG)
            m_new = jnp.maximum(m, jnp.max(s, axis=-1, keepdims=True))
            alpha = jnp.exp2(m - m_new)
            p = jnp.exp2(s - m_new)
            l = alpha * l + jnp.sum(p, axis=-1, keepdims=True)
            acc = alpha * acc + jnp.dot(p.astype(BF16), v, preferred_element_type=F32)
            return m_new, l, acc

        init = (jnp.full((tq, 1), NEG_BIG, F32), jnp.zeros((tq, 1), F32),
                jnp.zeros((tq, LANES), F32))
        carry = lax.fori_loop(0, i, functools.partial(step, masked=False), init)
        m, l, acc = step(i, carry, True)
        outs.append(acc / l)
    lane = lax.broadcasted_iota(jnp.int32, (tq, LANES), 1)
    o_ref[...] = jnp.where(lane < dh, outs[0], outs[1]).astype(o_ref.dtype)


def _attention(q_aug, k_aug, v, dh):
    b, s, _ = v.shape
    pairs = N_HEADS // 2
    return pl.pallas_call(
        functools.partial(_attn_kernel, dh=dh),
        grid=(b, pairs, s // TQ),
        in_specs=[pl.BlockSpec((None, TQ, 2 * LANES), lambda bi, hp, i: (bi, i, hp)),
                  pl.BlockSpec((None, s, 2 * LANES), lambda bi, hp, i: (bi, 0, hp)),
                  pl.BlockSpec((None, s, LANES), lambda bi, hp, i: (bi, 0, hp))],
        out_specs=pl.BlockSpec((None, TQ, LANES), lambda bi, hp, i: (bi, i, hp)),
        out_shape=jax.ShapeDtypeStruct(v.shape, BF16),
        compiler_params=_params("parallel", "parallel", "arbitrary"),
        name="fox_attention",
    )(q_aug, k_aug, v)


def _oproj_kernel(h_ref, o_ref, w_ref, out_ref):
    out_ref[...] = h_ref[...] + jnp.dot(o_ref[...], w_ref[...], preferred_element_type=F32)


def _oproj(h, o, w_o):
    n, d = h.shape
    row = pl.BlockSpec((TM, d), lambda i: (i, 0))
    return pl.pallas_call(
        _oproj_kernel,
        grid=(n // TM,),
        in_specs=[row, row, _resident((d, d))],
        out_specs=row,
        out_shape=jax.ShapeDtypeStruct((n, d), F32),
        compiler_params=_params("parallel"),
        name="o_proj",
    )(h, o, w_o.astype(BF16))


def _final_norm_kernel(h_ref, g_ref, o_ref):
    o_ref[...] = _rms(h_ref[...], g_ref[...])


def _final_norm(h, g):
    n, d = h.shape
    row = pl.BlockSpec((TM, d), lambda i: (i, 0))
    return pl.pallas_call(
        _final_norm_kernel,
        grid=(n // TM,),
        in_specs=[row, _resident((1, d))],
        out_specs=row,
        out_shape=jax.ShapeDtypeStruct((n, d), F32),
        compiler_params=_params("parallel"),
        name="final_norm",
    )(h, g.reshape(1, d))


def kernel(x, p, mix_norm, conv_w_pw1, conv_b_pw1, conv_w_dw, conv_b_dw, conv_ln_g, conv_ln_b,
           conv_w_pw2, conv_b_pw2, kv_norm, w_kvf, b_f, attn_w_q, attn_w_o, ffn_norm, ffn_w1,
           ffn_w2, ple_norm, ple_w_gate, ple_w_proj, final_norm):
    b, s, d = x.shape
    n = b * s
    depth = mix_norm.shape[0]
    n_a = conv_w_pw1.shape[0]
    dh = d // N_HEADS
    assert s % TQ == 0 and s % TM == 0 and s % TM_CONV == 0 and d % (2 * LANES) == 0

    h = x.reshape(n, d)
    c = k_aug = v = None
    for i in range(depth):
        if i < n_a:
            glu = _pw1_glu(h, mix_norm[i], conv_w_pw1[i], conv_b_pw1[i])
            h = _conv_block(glu.reshape(b, s, d), h.reshape(b, s, d), conv_w_dw[i], conv_b_dw[i],
                            conv_ln_g[i], conv_ln_b[i], conv_w_pw2[i], conv_b_pw2[i]).reshape(n, d)
        else:
            j = i - n_a
            if j == 0:
                c = _gates(h.reshape(b, s, d), kv_norm, w_kvf[:, 2 * d:], b_f).reshape(n, LANES)
                k_aug, v = _kv(h, c, kv_norm, w_kvf[:, :d], w_kvf[:, d:2 * d])
            q_aug = _q(h, c, mix_norm[i], attn_w_q[j])
            o = _attention(q_aug.reshape(b, s, -1), k_aug.reshape(b, s, -1), v.reshape(b, s, d), dh)
            h = _oproj(h, o.reshape(n, d), attn_w_o[j])
        h = _ffn(h, ffn_norm[i], ffn_w1[i], ffn_w2[i])
        h = _ple(h, p[i].reshape(n, -1), ple_norm[i], ple_w_gate[i], ple_w_proj[i])
    return _final_norm(h, final_norm).reshape(b, s, d)
```

```python
import functools
import math

import jax
import jax.numpy as jnp
from jax import lax
from jax.experimental import pallas as pl
from jax.experimental.pallas import tpu as pltpu

F32 = jnp.float32
BF16 = jnp.bfloat16

EPS = 1e-6
NEG_BIG = -1e30
N_HEADS = 16
CONV_WIDTH = 31
LOG2E = math.log2(math.e)

LANES = 128
HALO = 32
TM = 512
TM_CONV = 256
TQ = 512
VMEM_LIMIT = 56 * 1024 * 1024


def _params(*sem):
    return pltpu.CompilerParams(dimension_semantics=sem, vmem_limit_bytes=VMEM_LIMIT)


def _resident(shape):
    nd = len(shape)
    return pl.BlockSpec(shape, lambda *_: (0,) * nd)


def _rms(x, g):
    return x * lax.rsqrt(jnp.mean(x * x, axis=-1, keepdims=True) + EPS) * g


def _split3(x):
    hi = x.astype(BF16)
    r = x - hi.astype(F32)
    mid = r.astype(BF16)
    lo = (r - mid.astype(F32)).astype(BF16)
    return hi, mid, lo


def _pw1_glu_kernel(h_ref, g_ref, wa_ref, wg_ref, ba_ref, bg_ref, o_ref):
    xn = _rms(h_ref[...], g_ref[...]).astype(BF16)
    a = jnp.dot(xn, wa_ref[...], preferred_element_type=F32) + ba_ref[...]
    gt = jnp.dot(xn, wg_ref[...], preferred_element_type=F32) + bg_ref[...]
    o_ref[...] = (a * jax.nn.sigmoid(gt)).astype(o_ref.dtype)


def _pw1_glu(h, g, w, b):
    n, d = h.shape
    wa, wg = w[:, :d].astype(BF16), w[:, d:].astype(BF16)
    ba, bg = b[:d].reshape(1, d), b[d:].reshape(1, d)
    row = pl.BlockSpec((TM, d), lambda i: (i, 0))
    return pl.pallas_call(
        _pw1_glu_kernel,
        grid=(n // TM,),
        in_specs=[row, _resident((1, d)), _resident((d, d)), _resident((d, d)),
                  _resident((1, d)), _resident((1, d))],
        out_specs=row,
        out_shape=jax.ShapeDtypeStruct((n, d), BF16),
        compiler_params=_params("parallel"),
        name="pw1_glu",
    )(h, g.reshape(1, d), wa, wg, ba, bg)


def _conv_kernel(x_ref, halo_ref, h_ref, wdw_ref, bdw_ref, lg_ref, lb_ref, w2_ref, b2_ref,
                 o_ref, xs_ref, y_ref):
    tm, d = x_ref.shape
    first = pl.program_id(1) == 0
    xs_ref[0:HALO, :] = jnp.where(first, 0.0, halo_ref[...].astype(F32))
    xs_ref[HALO:, :] = x_ref[...].astype(F32)

    groups = [[] for _ in range(8)]
    for k in range(CONV_WIDTH):
        a, b = divmod(HALO - (CONV_WIDTH - 1) + k, 8)
        groups[b].append((a, k))
    row = lax.broadcasted_iota(jnp.int32, (8, LANES), 0)

    def strip(c, carry):
        lanes = pl.ds(pl.multiple_of(c * LANES, LANES), LANES)
        w = [jnp.broadcast_to(wdw_ref[k:k + 1, lanes], (8, LANES)) for k in range(CONV_WIDTH)]
        bias = jnp.broadcast_to(bdw_ref[:, lanes], (8, LANES))

        def u(b, chunk):
            acc = None
            for a, k in groups[b]:
                term = w[k] * xs_ref[(chunk + a) * 8:(chunk + a + 1) * 8, lanes]
                acc = term if acc is None else acc + term
            return acc

        def rotated(chunk):
            return [pltpu.roll(u(b, chunk), 8 - b, 0) for b in range(1, 8)]

        cur = rotated(0)
        for chunk in range(tm // 8):
            nxt = rotated(chunk + 1)
            acc = bias + u(0, chunk)
            for b in range(1, 8):
                acc = acc + jnp.where(row < 8 - b, cur[b - 1], nxt[b - 1])
            y_ref[chunk * 8:(chunk + 1) * 8, lanes] = acc
            cur = nxt
        return carry

    lax.fori_loop(0, d // LANES, strip, 0)

    y = y_ref[...]
    mu = jnp.mean(y, axis=-1, keepdims=True)
    yc = y - mu
    var = jnp.mean(yc * yc, axis=-1, keepdims=True)
    z = yc * lax.rsqrt(var + EPS) * lg_ref[...] + lb_ref[...]
    z = z * jax.nn.sigmoid(z)
    out = jnp.dot(z.astype(BF16), w2_ref[...], preferred_element_type=F32) + b2_ref[...]
    o_ref[...] = h_ref[...] + out


def _conv_block(glu, h, w_dw, b_dw, ln_g, ln_b, w_pw2, b_pw2):
    b, s, d = glu.shape
    tm = TM_CONV
    per = tm // HALO
    wdw = jnp.zeros((HALO, d), F32).at[:CONV_WIDTH].set(w_dw)
    row = pl.BlockSpec((None, tm, d), lambda bi, i: (bi, i, 0))
    halo = pl.BlockSpec((None, HALO, d), lambda bi, i: (bi, jnp.maximum(i * per - 1, 0), 0))
    vec = _resident((1, d))
    return pl.pallas_call(
        _conv_kernel,
        grid=(b, s // tm),
        in_specs=[row, halo, row, _resident((HALO, d)), vec, vec, vec, _resident((d, d)), vec],
        out_specs=row,
        out_shape=jax.ShapeDtypeStruct((b, s, d), F32),
        scratch_shapes=[pltpu.VMEM((tm + HALO, d), F32), pltpu.VMEM((tm, d), F32)],
        compiler_params=_params("parallel", "parallel"),
        name="dwconv_ln_pw2",
    )(glu, glu, h, wdw, b_dw.reshape(1, d), ln_g.reshape(1, d), ln_b.reshape(1, d),
      w_pw2.astype(BF16), b_pw2.reshape(1, d))


def _ffn_kernel(h_ref, g_ref, w1_ref, w2_ref, o_ref, *, chunk):
    h = h_ref[...]
    xn = _rms(h, g_ref[...]).astype(BF16)
    acc = h
    for c in range(w1_ref.shape[1] // chunk):
        a = jnp.dot(xn, w1_ref[:, c * chunk:(c + 1) * chunk], preferred_element_type=F32)
        a = jnp.square(jnp.maximum(a, 0.0)).astype(BF16)
        acc = acc + jnp.dot(a, w2_ref[c * chunk:(c + 1) * chunk, :], preferred_element_type=F32)
    o_ref[...] = acc


def _ffn(h, g, w1, w2):
    n, d = h.shape
    dff = w1.shape[1]
    row = pl.BlockSpec((TM, d), lambda i: (i, 0))
    return pl.pallas_call(
        functools.partial(_ffn_kernel, chunk=1024),
        grid=(n // TM,),
        in_specs=[row, _resident((1, d)), _resident((d, dff)), _resident((dff, d))],
        out_specs=row,
        out_shape=jax.ShapeDtypeStruct((n, d), F32),
        compiler_params=_params("parallel"),
        name="ffn",
    )(h, g.reshape(1, d), w1.astype(BF16), w2.astype(BF16))


def _ple_kernel(h_ref, p_ref, g_ref, wg_ref, wp_ref, o_ref):
    h = h_ref[...]
    xn = _rms(h, g_ref[...]).astype(BF16)
    gate = jax.nn.sigmoid(jnp.dot(xn, wg_ref[...], preferred_element_type=F32))
    proj = jnp.dot(p_ref[...].astype(BF16), wp_ref[...], preferred_element_type=F32)
    o_ref[...] = h + gate * proj


def _ple(h, p, g, w_gate, w_proj):
    n, d = h.shape
    dp = p.shape[1]
    row = pl.BlockSpec((TM, d), lambda i: (i, 0))
    return pl.pallas_call(
        _ple_kernel,
        grid=(n // TM,),
        in_specs=[row, pl.BlockSpec((TM, dp), lambda i: (i, 0)), _resident((1, d)),
                  _resident((d, d)), _resident((dp, d))],
        out_specs=row,
        out_shape=jax.ShapeDtypeStruct((n, d), F32),
        compiler_params=_params("parallel"),
        name="ple",
    )(h, p, g.reshape(1, d), w_gate.astype(BF16), w_proj.astype(BF16))


def _gate_kernel(h_ref, g_ref, wf_ref, bf_ref, c_ref, carry_ref):
    @pl.when(pl.program_id(1) == 0)
    def _():
        carry_ref[...] = jnp.zeros_like(carry_ref)

    tm = h_ref.shape[0]
    xn = _rms(h_ref[...], g_ref[...]).astype(BF16)
    f = jnp.dot(xn, wf_ref[...], preferred_element_type=F32) + bf_ref[...]
    logf = jnp.minimum(f, 0.0) - jnp.log1p(jnp.exp(-jnp.abs(f)))
    rows = lax.broadcasted_iota(jnp.int32, (tm, tm), 0)
    cols = lax.broadcasted_iota(jnp.int32, (tm, tm), 1)
    tri = jnp.where(rows >= cols, 1.0, 0.0).astype(BF16)
    hi, mid, lo = _split3(logf)
    c = (jnp.dot(tri, hi, preferred_element_type=F32)
         + jnp.dot(tri, mid, preferred_element_type=F32)
         + jnp.dot(tri, lo, preferred_element_type=F32)) + carry_ref[...]
    c_ref[...] = c
    carry_ref[...] = c[tm - 1:tm, :]


def _gates(h, g, w_f, b_f):
    b, s, d = h.shape
    nh = w_f.shape[1]
    wf = jnp.zeros((d, LANES), F32).at[:, :nh].set(w_f).astype(BF16)
    bf = jnp.zeros((1, LANES), F32).at[0, :nh].set(b_f)
    return pl.pallas_call(
        _gate_kernel,
        grid=(b, s // TM),
        in_specs=[pl.BlockSpec((None, TM, d), lambda bi, i: (bi, i, 0)), _resident((1, d)),
                  _resident((d, LANES)), _resident((1, LANES))],
        out_specs=pl.BlockSpec((None, TM, LANES), lambda bi, i: (bi, i, 0)),
        out_shape=jax.ShapeDtypeStruct((b, s, LANES), F32),
        scratch_shapes=[pltpu.VMEM((1, LANES), F32)],
        compiler_params=_params("parallel", "arbitrary"),
        name="forget_gate_cumsum",
    )(h, g.reshape(1, d), wf, bf)


def _place(c2, e_ref):
    hi, mid, lo = _split3(c2)
    return (jnp.dot(hi, e_ref[0], preferred_element_type=F32)
            + jnp.dot(mid, e_ref[1], preferred_element_type=F32)
            + jnp.dot(lo, e_ref[2], preferred_element_type=F32))


def _kv_kernel(h_ref, c_ref, g_ref, wk_ref, wv_ref, e_ref, ones_ref, k_ref, v_ref):
    xn = _rms(h_ref[...], g_ref[...]).astype(BF16)
    k = jnp.dot(xn, wk_ref[...], preferred_element_type=F32)
    k = k + _place(-(c_ref[...] * LOG2E), e_ref) + ones_ref[...]
    k_ref[...] = k.astype(k_ref.dtype)
    v_ref[...] = jnp.dot(xn, wv_ref[...], preferred_element_type=F32).astype(v_ref.dtype)


def _q_kernel(h_ref, c_ref, g_ref, wq_ref, e_ref, ones_ref, q_ref, *, scale):
    xn = _rms(h_ref[...], g_ref[...]).astype(BF16)
    q = jnp.dot(xn, wq_ref[...], preferred_element_type=F32) * scale
    q = q + _place(c_ref[...] * LOG2E, e_ref) + ones_ref[...]
    q_ref[...] = q.astype(q_ref.dtype)


def _widen_cols(w, nh):
    d, dm = w.shape
    dh = dm // nh
    w = w.reshape(d, nh, dh)
    return jnp.concatenate([w, jnp.zeros((d, nh, LANES - dh), w.dtype)], axis=2).reshape(d, nh * LANES)


def _aug_consts(nh, dh, c_off, one_off):
    e = jnp.zeros((3, LANES, nh * LANES), F32)
    ones = jnp.zeros((1, nh * LANES), F32)
    heads = jnp.arange(nh)
    for r in range(3):
        e = e.at[r, heads, heads * LANES + dh + c_off + r].set(1.0)
        ones = ones.at[0, heads * LANES + dh + one_off + r].set(1.0)
    return e.astype(BF16), ones


def _kv(h, c, g, w_k, w_v):
    n, d = h.shape
    nh = N_HEADS
    dh = d // nh
    e, ones = _aug_consts(nh, dh, 3, 0)
    row = pl.BlockSpec((TM, d), lambda i: (i, 0))
    wide = pl.BlockSpec((TM, nh * LANES), lambda i: (i, 0))
    return pl.pallas_call(
        _kv_kernel,
        grid=(n // TM,),
        in_specs=[row, pl.BlockSpec((TM, LANES), lambda i: (i, 0)), _resident((1, d)),
                  _resident((d, nh * LANES)), _resident((d, d)), _resident((3, LANES, nh * LANES)),
                  _resident((1, nh * LANES))],
        out_specs=[wide, row],
        out_shape=[jax.ShapeDtypeStruct((n, nh * LANES), BF16), jax.ShapeDtypeStruct((n, d), BF16)],
        compiler_params=_params("parallel"),
        name="kv_proj",
    )(h, c, g.reshape(1, d), _widen_cols(w_k, nh).astype(BF16), w_v.astype(BF16), e, ones)


def _q(h, c, g, w_q):
    n, d = h.shape
    nh = N_HEADS
    dh = d // nh
    e, ones = _aug_consts(nh, dh, 0, 3)
    row = pl.BlockSpec((TM, d), lambda i: (i, 0))
    wide = pl.BlockSpec((TM, nh * LANES), lambda i: (i, 0))
    return pl.pallas_call(
        functools.partial(_q_kernel, scale=dh ** -0.5 * LOG2E),
        grid=(n // TM,),
        in_specs=[row, pl.BlockSpec((TM, LANES), lambda i: (i, 0)), _resident((1, d)),
                  _resident((d, nh * LANES)), _resident((3, LANES, nh * LANES)),
                  _resident((1, nh * LANES))],
        out_specs=wide,
        out_shape=jax.ShapeDtypeStruct((n, nh * LANES), BF16),
        compiler_params=_params("parallel"),
        name="q_proj",
    )(h, c, g.reshape(1, d), _widen_cols(w_q, nh).astype(BF16), e, ones)


def _attn_kernel(qt_ref, k_ref, vt_ref, o_ref, s_ref, *, dh):
    tq = qt_ref.shape[1]
    tk = s_ref.shape[2]
    assert tq == 2 * tk
    i = pl.program_id(2)
    kpos = lax.broadcasted_iota(jnp.int32, (tk, tq), 0)
    qpos = lax.broadcasted_iota(jnp.int32, (tk, tq), 1)

    def scores(hh, j):
        kv = pl.ds(pl.multiple_of(j * tk, tk), tk)
        k = k_ref[kv, hh * LANES:(hh + 1) * LANES]
        qt = qt_ref[hh * LANES:(hh + 1) * LANES, :]
        return jnp.dot(k, qt, preferred_element_type=F32)

    def update(hh, j, s, carry, diag):
        m, l, acc = carry
        kv = pl.ds(pl.multiple_of(j * tk, tk), tk)
        vt = vt_ref[hh * dh:(hh + 1) * dh, kv]
        if diag is not None:
            s = jnp.where(kpos + diag * tk <= qpos, s, NEG_BIG)
        m_new = jnp.maximum(m, jnp.max(s, axis=0, keepdims=True))
        alpha = jnp.exp2(m - m_new)
        p = jnp.exp2(s - m_new)
        l = alpha * l + jnp.sum(p, axis=0, keepdims=True)
        acc = alpha * acc + jnp.dot(vt, p.astype(BF16), preferred_element_type=F32)
        return m_new, l, acc

    def half(j, slot, stats, diag):
        new = []
        for hh in range(2):
            s_ref[1 - slot, hh] = scores(hh, j + 1)
            new.append(update(hh, j, s_ref[slot, hh], stats[hh], diag))
        return tuple(new)

    def step(t, stats):
        stats = half(2 * t, 0, stats, None)
        return half(2 * t + 1, 1, stats, None)

    for hh in range(2):
        s_ref[0, hh] = scores(hh, 0)
    init = (jnp.full((1, tq), NEG_BIG, F32), jnp.zeros((1, tq), F32), jnp.zeros((dh, tq), F32))
    stats = lax.fori_loop(0, i, step, (init, init))
    stats = half(2 * i, 0, stats, 0)
    stats = [update(hh, 2 * i + 1, s_ref[1, hh], stats[hh], 1) for hh in range(2)]
    outs = [acc / l for (_, l, acc) in stats]
    o_ref[...] = jnp.concatenate(outs, axis=0).T.astype(o_ref.dtype)


def _attention(qt_aug, k_aug, vt, dh):
    b, d, s = vt.shape
    pairs = N_HEADS // 2
    return pl.pallas_call(
        functools.partial(_attn_kernel, dh=dh),
        grid=(b, pairs, s // TQ),
        in_specs=[pl.BlockSpec((None, 2 * LANES, TQ), lambda bi, hp, i: (bi, hp, i)),
                  pl.BlockSpec((None, s, 2 * LANES), lambda bi, hp, i: (bi, 0, hp)),
                  pl.BlockSpec((None, 2 * dh, s), lambda bi, hp, i: (bi, hp, 0))],
        out_specs=pl.BlockSpec((None, TQ, 2 * dh), lambda bi, hp, i: (bi, i, hp)),
        out_shape=jax.ShapeDtypeStruct((b, s, d), BF16),
        scratch_shapes=[pltpu.VMEM((2, 2, TQ // 2, TQ), F32)],
        compiler_params=_params("parallel", "parallel", "arbitrary"),
        name="fox_attention",
    )(qt_aug, k_aug, vt)


def _oproj_kernel(h_ref, o_ref, w_ref, out_ref):
    out_ref[...] = h_ref[...] + jnp.dot(o_ref[...], w_ref[...], preferred_element_type=F32)


def _oproj(h, o, w_o):
    n, d = h.shape
    row = pl.BlockSpec((TM, d), lambda i: (i, 0))
    return pl.pallas_call(
        _oproj_kernel,
        grid=(n // TM,),
        in_specs=[row, row, _resident((d, d))],
        out_specs=row,
        out_shape=jax.ShapeDtypeStruct((n, d), F32),
        compiler_params=_params("parallel"),
        name="o_proj",
    )(h, o, w_o.astype(BF16))


def _final_norm_kernel(h_ref, g_ref, o_ref):
    o_ref[...] = _rms(h_ref[...], g_ref[...])


def _final_norm(h, g):
    n, d = h.shape
    row = pl.BlockSpec((TM, d), lambda i: (i, 0))
    return pl.pallas_call(
        _final_norm_kernel,
        grid=(n // TM,),
        in_specs=[row, _resident((1, d))],
        out_specs=row,
        out_shape=jax.ShapeDtypeStruct((n, d), F32),
        compiler_params=_params("parallel"),
        name="final_norm",
    )(h, g.reshape(1, d))


def kernel(x, p, mix_norm, conv_w_pw1, conv_b_pw1, conv_w_dw, conv_b_dw, conv_ln_g, conv_ln_b,
           conv_w_pw2, conv_b_pw2, kv_norm, w_kvf, b_f, attn_w_q, attn_w_o, ffn_norm, ffn_w1,
           ffn_w2, ple_norm, ple_w_gate, ple_w_proj, final_norm):
    b, s, d = x.shape
    n = b * s
    depth = mix_norm.shape[0]
    n_a = conv_w_pw1.shape[0]
    dh = d // N_HEADS
    assert s % TQ == 0 and s % TM == 0 and s % TM_CONV == 0 and d % (2 * LANES) == 0

    h = x.reshape(n, d)
    c = k_aug = v = None
    for i in range(depth):
        if i < n_a:
            glu = _pw1_glu(h, mix_norm[i], conv_w_pw1[i], conv_b_pw1[i])
            h = _conv_block(glu.reshape(b, s, d), h.reshape(b, s, d), conv_w_dw[i], conv_b_dw[i],
                            conv_ln_g[i], conv_ln_b[i], conv_w_pw2[i], conv_b_pw2[i]).reshape(n, d)
        else:
            j = i - n_a
            if j == 0:
                c = _gates(h.reshape(b, s, d), kv_norm, w_kvf[:, 2 * d:], b_f).reshape(n, LANES)
                k_aug, v = _kv(h, c, kv_norm, w_kvf[:, :d], w_kvf[:, d:2 * d])
                k_aug = k_aug.reshape(b, s, -1)
                vt = jnp.swapaxes(v.reshape(b, s, d), 1, 2)
            qt_aug = jnp.swapaxes(_q(h, c, mix_norm[i], attn_w_q[j]).reshape(b, s, -1), 1, 2)
            o = _attention(qt_aug, k_aug, vt, dh)
            h = _oproj(h, o.reshape(n, d), attn_w_o[j])
        h = _ffn(h, ffn_norm[i], ffn_w1[i], ffn_w2[i])
        h = _ple(h, p[i].reshape(n, -1), ple_norm[i], ple_w_gate[i], ple_w_proj[i])
    return _final_norm(h, final_norm).reshape(b, s, d)
```

```python
import functools
import math

import jax
import jax.numpy as jnp
from jax import lax
from jax.experimental import pallas as pl
from jax.experimental.pallas import tpu as pltpu

F32 = jnp.float32
BF16 = jnp.bfloat16

EPS = 1e-6
NEG_BIG = -1e30
N_HEADS = 16
CONV_WIDTH = 31
LOG2E = math.log2(math.e)

LANES = 128
BF16_ROWS = 16
HALO = 32
TM = 512
TM_CONV = 256
TQ = 512
ATTN_HEADS_PER_STEP = 4
VMEM_LIMIT = 56 * 1024 * 1024

_NT = (((1,), (1,)), ((), ()))


def _params(*sem):
    return pltpu.CompilerParams(dimension_semantics=sem, vmem_limit_bytes=VMEM_LIMIT)


def _resident(shape):
    nd = len(shape)
    return pl.BlockSpec(shape, lambda *_: (0,) * nd, pipeline_mode=pl.Buffered(1))


def _rms(x, g):
    return x * lax.rsqrt(jnp.mean(x * x, axis=-1, keepdims=True) + EPS) * g


def _split3(x):
    hi = x.astype(BF16)
    r = x - hi.astype(F32)
    mid = r.astype(BF16)
    lo = (r - mid.astype(F32)).astype(BF16)
    return hi, mid, lo


def _pw1_glu_kernel(h_ref, g_ref, wa_ref, wg_ref, ba_ref, bg_ref, o_ref):
    xn = _rms(h_ref[...], g_ref[...]).astype(BF16)
    a = jnp.dot(xn, wa_ref[...], preferred_element_type=F32) + ba_ref[...]
    gt = jnp.dot(xn, wg_ref[...], preferred_element_type=F32) + bg_ref[...]
    o_ref[...] = (a * jax.nn.sigmoid(gt)).astype(o_ref.dtype)


def _pw1_glu(h, g, w, b):
    n, d = h.shape
    wa, wg = w[:, :d].astype(BF16), w[:, d:].astype(BF16)
    ba, bg = b[:d].reshape(1, d), b[d:].reshape(1, d)
    row = pl.BlockSpec((TM, d), lambda i: (i, 0))
    return pl.pallas_call(
        _pw1_glu_kernel,
        grid=(n // TM,),
        in_specs=[row, _resident((1, d)), _resident((d, d)), _resident((d, d)),
                  _resident((1, d)), _resident((1, d))],
        out_specs=row,
        out_shape=jax.ShapeDtypeStruct((n, d), BF16),
        compiler_params=_params("parallel"),
        name="pw1_glu",
    )(h, g.reshape(1, d), wa, wg, ba, bg)


def _conv_kernel(x_ref, halo_ref, h_ref, wdw_ref, bdw_ref, lg_ref, lb_ref, w2_ref, b2_ref,
                 o_ref, xs_ref, y_ref):
    tm, d = x_ref.shape
    first = pl.program_id(1) == 0
    xs_ref[0:HALO, :] = jnp.where(first, 0.0, halo_ref[...].astype(F32))
    xs_ref[HALO:, :] = x_ref[...].astype(F32)

    groups = [[] for _ in range(8)]
    for k in range(CONV_WIDTH):
        a, b = divmod(HALO - (CONV_WIDTH - 1) + k, 8)
        groups[b].append((a, k))
    row = lax.broadcasted_iota(jnp.int32, (8, LANES), 0)

    def strip(c, carry):
        lanes = pl.ds(pl.multiple_of(c * LANES, LANES), LANES)
        w = [jnp.broadcast_to(wdw_ref[k:k + 1, lanes], (8, LANES)) for k in range(CONV_WIDTH)]
        bias = jnp.broadcast_to(bdw_ref[:, lanes], (8, LANES))

        def u(b, chunk):
            acc = None
            for a, k in groups[b]:
                term = w[k] * xs_ref[(chunk + a) * 8:(chunk + a + 1) * 8, lanes]
                acc = term if acc is None else acc + term
            return acc

        def rotated(chunk):
            return [pltpu.roll(u(b, chunk), 8 - b, 0) for b in range(1, 8)]

        cur = rotated(0)
        for chunk in range(tm // 8):
            nxt = rotated(chunk + 1)
            acc = bias + u(0, chunk)
            for b in range(1, 8):
                acc = acc + jnp.where(row < 8 - b, cur[b - 1], nxt[b - 1])
            y_ref[chunk * 8:(chunk + 1) * 8, lanes] = acc
            cur = nxt
        return carry

    lax.fori_loop(0, d // LANES, strip, 0)

    y = y_ref[...]
    mu = jnp.mean(y, axis=-1, keepdims=True)
    yc = y - mu
    var = jnp.mean(yc * yc, axis=-1, keepdims=True)
    z = yc * lax.rsqrt(var + EPS) * lg_ref[...] + lb_ref[...]
    z = z * jax.nn.sigmoid(z)
    out = jnp.dot(z.astype(BF16), w2_ref[...], preferred_element_type=F32) + b2_ref[...]
    o_ref[...] = h_ref[...] + out


def _conv_block(glu, h, w_dw, b_dw, ln_g, ln_b, w_pw2, b_pw2):
    b, s, d = glu.shape
    tm = TM_CONV
    per = tm // HALO
    wdw = jnp.zeros((HALO, d), F32).at[:CONV_WIDTH].set(w_dw)
    row = pl.BlockSpec((None, tm, d), lambda bi, i: (bi, i, 0))
    halo = pl.BlockSpec((None, HALO, d), lambda bi, i: (bi, jnp.maximum(i * per - 1, 0), 0))
    vec = _resident((1, d))
    return pl.pallas_call(
        _conv_kernel,
        grid=(b, s // tm),
        in_specs=[row, halo, row, _resident((HALO, d)), vec, vec, vec, _resident((d, d)), vec],
        out_specs=row,
        out_shape=jax.ShapeDtypeStruct((b, s, d), F32),
        scratch_shapes=[pltpu.VMEM((tm + HALO, d), F32), pltpu.VMEM((tm, d), F32)],
        compiler_params=_params("parallel", "parallel"),
        name="dwconv_ln_pw2",
    )(glu, glu, h, wdw, b_dw.reshape(1, d), ln_g.reshape(1, d), ln_b.reshape(1, d),
      w_pw2.astype(BF16), b_pw2.reshape(1, d))


def _ffn_ple_kernel(*refs, chunk, with_attn, with_final):
    refs = list(refs)
    h_ref = refs.pop(0)
    h = h_ref[...]
    if with_attn:
        o_ref, wo_ref = refs.pop(0), refs.pop(0)
        h = h + jnp.dot(o_ref[...], wo_ref[...], preferred_element_type=F32)
    p_ref, g1_ref, w1_ref, w2_ref, g2_ref, wg_ref, wp_ref = refs[:7]
    refs = refs[7:]
    xn = _rms(h, g1_ref[...]).astype(BF16)
    for c in range(w1_ref.shape[1] // chunk):
        a = jnp.dot(xn, w1_ref[:, c * chunk:(c + 1) * chunk], preferred_element_type=F32)
        a = jnp.square(jnp.maximum(a, 0.0)).astype(BF16)
        h = h + jnp.dot(a, w2_ref[c * chunk:(c + 1) * chunk, :], preferred_element_type=F32)
    xn = _rms(h, g2_ref[...]).astype(BF16)
    gate = jax.nn.sigmoid(jnp.dot(xn, wg_ref[...], preferred_element_type=F32))
    proj = jnp.dot(p_ref[...].astype(BF16), wp_ref[...], preferred_element_type=F32)
    h = h + gate * proj
    if with_final:
        gf_ref = refs.pop(0)
        h = _rms(h, gf_ref[...])
    out_ref, = refs
    out_ref[...] = h


def _ffn_ple(h, p, g1, w1, w2, g2, w_gate, w_proj, attn=None, final_g=None):
    n, d = h.shape
    dff = w1.shape[1]
    dp = p.shape[1]
    row = pl.BlockSpec((TM, d), lambda i: (i, 0))
    vec = _resident((1, d))
    args, specs = [h], [row]
    if attn is not None:
        o, w_o = attn
        args += [o, w_o.astype(BF16)]
        specs += [row, _resident((d, d))]
    args += [p, g1.reshape(1, d), w1.astype(BF16), w2.astype(BF16), g2.reshape(1, d),
             w_gate.astype(BF16), w_proj.astype(BF16)]
    specs += [pl.BlockSpec((TM, dp), lambda i: (i, 0)), vec, _resident((d, dff)),
              _resident((dff, d)), vec, _resident((d, d)), _resident((dp, d))]
    if final_g is not None:
        args.append(final_g.reshape(1, d))
        specs.append(vec)
    return pl.pallas_call(
        functools.partial(_ffn_ple_kernel, chunk=1024, with_attn=attn is not None,
                          with_final=final_g is not None),
        grid=(n // TM,),
        in_specs=specs,
        out_specs=row,
        out_shape=jax.ShapeDtypeStruct((n, d), F32),
        compiler_params=_params("parallel"),
        name="ffn_ple",
    )(*args)


def _gate_kernel(h_ref, g_ref, wf_ref, bf_ref, e_ref, ones_ref, c_ref, qa_ref, carry_ref):
    @pl.when(pl.program_id(1) == 0)
    def _():
        carry_ref[...] = jnp.zeros_like(carry_ref)

    tm = h_ref.shape[0]
    xn = _rms(h_ref[...], g_ref[...]).astype(BF16)
    f = jnp.dot(xn, wf_ref[...], preferred_element_type=F32) + bf_ref[...]
    logf = jnp.minimum(f, 0.0) - jnp.log1p(jnp.exp(-jnp.abs(f)))
    rows = lax.broadcasted_iota(jnp.int32, (tm, tm), 0)
    cols = lax.broadcasted_iota(jnp.int32, (tm, tm), 1)
    tri = jnp.where(rows >= cols, 1.0, 0.0).astype(BF16)
    c = sum(jnp.dot(tri, piece, preferred_element_type=F32) for piece in _split3(logf))
    c = c + carry_ref[...]
    c_ref[...] = c
    carry_ref[...] = c[tm - 1:tm, :]
    qa = sum(lax.dot_general(e_ref[r], piece, _NT, preferred_element_type=F32)
             for r, piece in enumerate(_split3(c * LOG2E)))
    qa_ref[...] = (qa + ones_ref[...]).astype(qa_ref.dtype)


def _gates(h, g, w_f, b_f):
    b, s, d = h.shape
    nh = w_f.shape[1]
    wf = jnp.zeros((d, LANES), F32).at[:, :nh].set(w_f).astype(BF16)
    bf = jnp.zeros((1, LANES), F32).at[0, :nh].set(b_f)
    heads = jnp.arange(nh)
    e = jnp.zeros((3, nh * BF16_ROWS, LANES), F32)
    ones = jnp.zeros((nh * BF16_ROWS, 1), F32)
    for r in range(3):
        e = e.at[r, heads * BF16_ROWS + r, heads].set(1.0)
        ones = ones.at[heads * BF16_ROWS + 3 + r, 0].set(1.0)
    return pl.pallas_call(
        _gate_kernel,
        grid=(b, s // TM),
        in_specs=[pl.BlockSpec((None, TM, d), lambda bi, i: (bi, i, 0)), _resident((1, d)),
                  _resident((d, LANES)), _resident((1, LANES)),
                  _resident((3, nh * BF16_ROWS, LANES)), _resident((nh * BF16_ROWS, 1))],
        out_specs=[pl.BlockSpec((None, TM, LANES), lambda bi, i: (bi, i, 0)),
                   pl.BlockSpec((None, nh * BF16_ROWS, TM), lambda bi, i: (bi, 0, i))],
        out_shape=[jax.ShapeDtypeStruct((b, s, LANES), F32),
                   jax.ShapeDtypeStruct((b, nh * BF16_ROWS, s), BF16)],
        scratch_shapes=[pltpu.VMEM((1, LANES), F32)],
        compiler_params=_params("parallel", "arbitrary"),
        name="forget_gate_cumsum",
    )(h, g.reshape(1, d), wf, bf, e.astype(BF16), ones)


def _kv_kernel(h_ref, c_ref, g_ref, wk_ref, wvt_ref, e_ref, ones_ref, k_ref, vt_ref):
    xn = _rms(h_ref[...], g_ref[...]).astype(BF16)
    k = jnp.dot(xn, wk_ref[...], preferred_element_type=F32)
    k = k + sum(jnp.dot(piece, e_ref[r], preferred_element_type=F32)
                for r, piece in enumerate(_split3(-(c_ref[...] * LOG2E)))) + ones_ref[...]
    k_ref[...] = k.astype(k_ref.dtype)
    vt_ref[...] = lax.dot_general(wvt_ref[...], xn, _NT,
                                  preferred_element_type=F32).astype(vt_ref.dtype)


def _kv(h, c, g, w_k, w_v):
    b, s, d = h.shape
    nh = N_HEADS
    dh = d // nh
    wk = w_k.reshape(d, nh, dh)
    wk = jnp.concatenate([wk, jnp.zeros((d, nh, LANES - dh), wk.dtype)], axis=2)
    heads = jnp.arange(nh)
    e = jnp.zeros((3, LANES, nh * LANES), F32)
    ones = jnp.zeros((1, nh * LANES), F32)
    for r in range(3):
        ones = ones.at[0, heads * LANES + dh + r].set(1.0)
        e = e.at[r, heads, heads * LANES + dh + 3 + r].set(1.0)
    return pl.pallas_call(
        _kv_kernel,
        grid=(b, s // TM),
        in_specs=[pl.BlockSpec((None, TM, d), lambda bi, i: (bi, i, 0)),
                  pl.BlockSpec((None, TM, LANES), lambda bi, i: (bi, i, 0)), _resident((1, d)),
                  _resident((d, nh * LANES)), _resident((d, d)), _resident((3, LANES, nh * LANES)),
                  _resident((1, nh * LANES))],
        out_specs=[pl.BlockSpec((None, TM, nh * LANES), lambda bi, i: (bi, i, 0)),
                   pl.BlockSpec((None, d, TM), lambda bi, i: (bi, 0, i))],
        out_shape=[jax.ShapeDtypeStruct((b, s, nh * LANES), BF16),
                   jax.ShapeDtypeStruct((b, d, s), BF16)],
        compiler_params=_params("parallel", "parallel"),
        name="kv_proj",
    )(h, c, g.reshape(1, d), wk.reshape(d, nh * LANES).astype(BF16), w_v.T.astype(BF16),
      e.astype(BF16), ones)


def _q_kernel(h_ref, g_ref, wqt_ref, qt_ref, *, scale):
    xn = _rms(h_ref[...], g_ref[...]).astype(BF16)
    qt = lax.dot_general(wqt_ref[...], xn, _NT, preferred_element_type=F32) * scale
    qt_ref[...] = qt.astype(qt_ref.dtype)


def _q(h, g, w_q):
    b, s, d = h.shape
    return pl.pallas_call(
        functools.partial(_q_kernel, scale=(d // N_HEADS) ** -0.5 * LOG2E),
        grid=(b, s // TM),
        in_specs=[pl.BlockSpec((None, TM, d), lambda bi, i: (bi, i, 0)), _resident((1, d)),
                  _resident((d, d))],
        out_specs=pl.BlockSpec((None, d, TM), lambda bi, i: (bi, 0, i)),
        out_shape=jax.ShapeDtypeStruct((b, d, s), BF16),
        compiler_params=_params("parallel", "parallel"),
        name="q_proj",
    )(h, g.reshape(1, d), w_q.T.astype(BF16))


def _attn_kernel(qt_ref, qa_ref, k_ref, vt_ref, o_ref, s_ref, *, dh, nh):
    tq = qt_ref.shape[1]
    tk = s_ref.shape[2]
    assert tq == 2 * tk
    i = pl.program_id(2)
    kpos = lax.broadcasted_iota(jnp.int32, (tk, tq), 0)
    qpos = lax.broadcasted_iota(jnp.int32, (tk, tq), 1)
    ones = jnp.ones((BF16_ROWS, tk), BF16)
    pad = jnp.zeros((LANES - dh - BF16_ROWS, tq), BF16)

    def scores(hh, j):
        kv = pl.ds(pl.multiple_of(j * tk, tk), tk)
        k = k_ref[kv, hh * LANES:(hh + 1) * LANES]
        qt = jnp.concatenate([qt_ref[hh * dh:(hh + 1) * dh, :],
                              qa_ref[hh * BF16_ROWS:(hh + 1) * BF16_ROWS, :], pad], axis=0)
        return jnp.dot(k, qt, preferred_element_type=F32)

    def update(hh, j, s, carry, diag):
        m, acc = carry
        kv = pl.ds(pl.multiple_of(j * tk, tk), tk)
        vt = jnp.concatenate([vt_ref[hh * dh:(hh + 1) * dh, kv], ones], axis=0)
        if diag is not None:
            s = jnp.where(kpos + diag * tk <= qpos, s, NEG_BIG)
        m_new = jnp.maximum(m, jnp.max(s, axis=0, keepdims=True))
        alpha = jnp.exp2(m - m_new)
        p = jnp.exp2((s - m_new).astype(BF16))
        acc = alpha * acc + jnp.dot(vt, p, preferred_element_type=F32)
        return m_new, acc

    def half(j, slot, stats, diag):
        new = []
        for hh in range(nh):
            s_ref[1 - slot, hh] = scores(hh, j + 1)
            new.append(update(hh, j, s_ref[slot, hh], stats[hh], diag))
        return tuple(new)

    def step(t, stats):
        stats = half(2 * t, 0, stats, None)
        return half(2 * t + 1, 1, stats, None)

    for hh in range(nh):
        s_ref[0, hh] = scores(hh, 0)
    init = (jnp.full((1, tq), NEG_BIG, F32), jnp.zeros((dh + BF16_ROWS, tq), F32))
    stats = lax.fori_loop(0, i, step, (init,) * nh)
    stats = half(2 * i, 0, stats, 0)
    stats = [update(hh, 2 * i + 1, s_ref[1, hh], stats[hh], 1) for hh in range(nh)]
    outs = [acc[:dh] / acc[dh:dh + 1] for (_, acc) in stats]
    o_ref[...] = jnp.concatenate(outs, axis=0).T.astype(o_ref.dtype)


def _attention(qt, qa, k_aug, vt):
    b, d, s = vt.shape
    nh = ATTN_HEADS_PER_STEP
    dh = d // N_HEADS
    return pl.pallas_call(
        functools.partial(_attn_kernel, dh=dh, nh=nh),
        grid=(b, N_HEADS // nh, s // TQ),
        in_specs=[pl.BlockSpec((None, nh * dh, TQ), lambda bi, hg, i: (bi, hg, i)),
                  pl.BlockSpec((None, nh * BF16_ROWS, TQ), lambda bi, hg, i: (bi, hg, i)),
                  pl.BlockSpec((None, s, nh * LANES), lambda bi, hg, i: (bi, 0, hg)),
                  pl.BlockSpec((None, nh * dh, s), lambda bi, hg, i: (bi, hg, 0))],
        out_specs=pl.BlockSpec((None, TQ, nh * dh), lambda bi, hg, i: (bi, i, hg)),
        out_shape=jax.ShapeDtypeStruct((b, s, d), BF16),
        scratch_shapes=[pltpu.VMEM((2, nh, TQ // 2, TQ), F32)],
        compiler_params=_params("parallel", "parallel", "arbitrary"),
        name="fox_attention",
    )(qt, qa, k_aug, vt)


def kernel(x, p, mix_norm, conv_w_pw1, conv_b_pw1, conv_w_dw, conv_b_dw, conv_ln_g, conv_ln_b,
           conv_w_pw2, conv_b_pw2, kv_norm, w_kvf, b_f, attn_w_q, attn_w_o, ffn_norm, ffn_w1,
           ffn_w2, ple_norm, ple_w_gate, ple_w_proj, final_norm):
    b, s, d = x.shape
    n = b * s
    depth = mix_norm.shape[0]
    n_a = conv_w_pw1.shape[0]
    assert s % TQ == 0 and s % TM == 0 and s % TM_CONV == 0
    assert d == N_HEADS * (d // N_HEADS) and d // N_HEADS + BF16_ROWS <= LANES

    h = x.reshape(n, d)
    qa = k_aug = vt = None
    for i in range(depth):
        attn = None
        if i < n_a:
            glu = _pw1_glu(h, mix_norm[i], conv_w_pw1[i], conv_b_pw1[i])
            h = _conv_block(glu.reshape(b, s, d), h.reshape(b, s, d), conv_w_dw[i], conv_b_dw[i],
                            conv_ln_g[i], conv_ln_b[i], conv_w_pw2[i], conv_b_pw2[i]).reshape(n, d)
        else:
            j = i - n_a
            h3 = h.reshape(b, s, d)
            if j == 0:
                c, qa = _gates(h3, kv_norm, w_kvf[:, 2 * d:], b_f)
                k_aug, vt = _kv(h3, c, kv_norm, w_kvf[:, :d], w_kvf[:, d:2 * d])
            o = _attention(_q(h3, mix_norm[i], attn_w_q[j]), qa, k_aug, vt)
            attn = (o.reshape(n, d), attn_w_o[j])
        h = _ffn_ple(h, p[i].reshape(n, -1), ffn_norm[i], ffn_w1[i], ffn_w2[i], ple_norm[i],
                     ple_w_gate[i], ple_w_proj[i], attn=attn,
                     final_g=final_norm if i == depth - 1 else None)
    return h.reshape(b, s, d)
```

```python
import functools
import math

import jax
import jax.numpy as jnp
from jax import lax
from jax.experimental import pallas as pl
from jax.experimental.pallas import tpu as pltpu

F32 = jnp.float32
BF16 = jnp.bfloat16

EPS = 1e-6
NEG_BIG = -1e30
N_HEADS = 16
CONV_WIDTH = 31
LOG2E = math.log2(math.e)

LANES = 128
BF16_ROWS = 16
HALO = 32
TM = 512
TM_CONV = 256
TQ = 512
TK = 256
ATTN_HEADS_PER_STEP = 4
VMEM_LIMIT = 56 * 1024 * 1024

_NT = (((1,), (1,)), ((), ()))


def _params(*sem):
    return pltpu.CompilerParams(dimension_semantics=sem, vmem_limit_bytes=VMEM_LIMIT)


def _resident(shape):
    nd = len(shape)
    return pl.BlockSpec(shape, lambda *_: (0,) * nd, pipeline_mode=pl.Buffered(1))


def _rms(x, g):
    return x * lax.rsqrt(jnp.mean(x * x, axis=-1, keepdims=True) + EPS) * g


def _split3(x):
    hi = x.astype(BF16)
    r = x - hi.astype(F32)
    mid = r.astype(BF16)
    lo = (r - mid.astype(F32)).astype(BF16)
    return hi, mid, lo


def _pw1_glu_kernel(h_ref, g_ref, wa_ref, wg_ref, ba_ref, bg_ref, o_ref):
    xn = _rms(h_ref[...], g_ref[...]).astype(BF16)
    a = jnp.dot(xn, wa_ref[...], preferred_element_type=F32) + ba_ref[...]
    gt = jnp.dot(xn, wg_ref[...], preferred_element_type=F32) + bg_ref[...]
    o_ref[...] = (a * jax.nn.sigmoid(gt)).astype(o_ref.dtype)


def _pw1_glu(h, g, w, b):
    n, d = h.shape
    wa, wg = w[:, :d].astype(BF16), w[:, d:].astype(BF16)
    ba, bg = b[:d].reshape(1, d), b[d:].reshape(1, d)
    row = pl.BlockSpec((TM, d), lambda i: (i, 0))
    return pl.pallas_call(
        _pw1_glu_kernel,
        grid=(n // TM,),
        in_specs=[row, _resident((1, d)), _resident((d, d)), _resident((d, d)),
                  _resident((1, d)), _resident((1, d))],
        out_specs=row,
        out_shape=jax.ShapeDtypeStruct((n, d), BF16),
        compiler_params=_params("parallel"),
        name="pw1_glu",
    )(h, g.reshape(1, d), wa, wg, ba, bg)


def _conv_kernel(x_ref, halo_ref, h_ref, wdw_ref, bdw_ref, lg_ref, lb_ref, w2_ref, b2_ref,
                 o_ref, xs_ref, y_ref):
    tm, d = x_ref.shape
    first = pl.program_id(1) == 0
    xs_ref[0:HALO, :] = jnp.where(first, 0.0, halo_ref[...].astype(F32))
    xs_ref[HALO:, :] = x_ref[...].astype(F32)

    groups = [[] for _ in range(8)]
    for k in range(CONV_WIDTH):
        a, b = divmod(HALO - (CONV_WIDTH - 1) + k, 8)
        groups[b].append((a, k))
    row = lax.broadcasted_iota(jnp.int32, (8, LANES), 0)

    def strip(c, carry):
        lanes = pl.ds(pl.multiple_of(c * LANES, LANES), LANES)
        w = [jnp.broadcast_to(wdw_ref[k:k + 1, lanes], (8, LANES)) for k in range(CONV_WIDTH)]
        bias = jnp.broadcast_to(bdw_ref[:, lanes], (8, LANES))

        def u(b, chunk):
            acc = None
            for a, k in groups[b]:
                term = w[k] * xs_ref[(chunk + a) * 8:(chunk + a + 1) * 8, lanes]
                acc = term if acc is None else acc + term
            return acc

        def rotated(chunk):
            return [pltpu.roll(u(b, chunk), 8 - b, 0) for b in range(1, 8)]

        cur = rotated(0)
        for chunk in range(tm // 8):
            nxt = rotated(chunk + 1)
            acc = bias + u(0, chunk)
            for b in range(1, 8):
                acc = acc + jnp.where(row < 8 - b, cur[b - 1], nxt[b - 1])
            y_ref[chunk * 8:(chunk + 1) * 8, lanes] = acc
            cur = nxt
        return carry

    lax.fori_loop(0, d // LANES, strip, 0)

    y = y_ref[...]
    mu = jnp.mean(y, axis=-1, keepdims=True)
    yc = y - mu
    var = jnp.mean(yc * yc, axis=-1, keepdims=True)
    z = yc * lax.rsqrt(var + EPS) * lg_ref[...] + lb_ref[...]
    z = z * jax.nn.sigmoid(z)
    out = jnp.dot(z.astype(BF16), w2_ref[...], preferred_element_type=F32) + b2_ref[...]
    o_ref[...] = h_ref[...] + out


def _conv_block(glu, h, w_dw, b_dw, ln_g, ln_b, w_pw2, b_pw2):
    b, s, d = glu.shape
    tm = TM_CONV
    per = tm // HALO
    wdw = jnp.zeros((HALO, d), F32).at[:CONV_WIDTH].set(w_dw)
    row = pl.BlockSpec((None, tm, d), lambda bi, i: (bi, i, 0))
    halo = pl.BlockSpec((None, HALO, d), lambda bi, i: (bi, jnp.maximum(i * per - 1, 0), 0))
    vec = _resident((1, d))
    return pl.pallas_call(
        _conv_kernel,
        grid=(b, s // tm),
        in_specs=[row, halo, row, _resident((HALO, d)), vec, vec, vec, _resident((d, d)), vec],
        out_specs=row,
        out_shape=jax.ShapeDtypeStruct((b, s, d), F32),
        scratch_shapes=[pltpu.VMEM((tm + HALO, d), F32), pltpu.VMEM((tm, d), F32)],
        compiler_params=_params("parallel", "parallel"),
        name="dwconv_ln_pw2",
    )(glu, glu, h, wdw, b_dw.reshape(1, d), ln_g.reshape(1, d), ln_b.reshape(1, d),
      w_pw2.astype(BF16), b_pw2.reshape(1, d))


def _ffn_ple_kernel(*refs, chunk, with_attn, with_final):
    refs = list(refs)
    h_ref = refs.pop(0)
    h = h_ref[...]
    if with_attn:
        o_ref, wo_ref = refs.pop(0), refs.pop(0)
        h = h + jnp.dot(o_ref[...], wo_ref[...], preferred_element_type=F32)
    p_ref, g1_ref, w1_ref, w2_ref, g2_ref, wg_ref, wp_ref = refs[:7]
    refs = refs[7:]
    xn = _rms(h, g1_ref[...]).astype(BF16)
    for c in range(w1_ref.shape[1] // chunk):
        a = jnp.dot(xn, w1_ref[:, c * chunk:(c + 1) * chunk], preferred_element_type=F32)
        a = jnp.square(jnp.maximum(a, 0.0)).astype(BF16)
        h = h + jnp.dot(a, w2_ref[c * chunk:(c + 1) * chunk, :], preferred_element_type=F32)
    xn = _rms(h, g2_ref[...]).astype(BF16)
    gate = jax.nn.sigmoid(jnp.dot(xn, wg_ref[...], preferred_element_type=F32))
    proj = jnp.dot(p_ref[...].astype(BF16), wp_ref[...], preferred_element_type=F32)
    h = h + gate * proj
    if with_final:
        gf_ref = refs.pop(0)
        h = _rms(h, gf_ref[...])
    out_ref, = refs
    out_ref[...] = h


def _ffn_ple(h, p, layer, g1, w1, w2, g2, w_gate, w_proj, attn=None, final_g=None):
    n, d = h.shape
    dff = w1.shape[1]
    dp = p.shape[2]
    row = pl.BlockSpec((TM, d), lambda i: (i, 0))
    vec = _resident((1, d))
    args, specs = [h], [row]
    if attn is not None:
        o, w_o = attn
        args += [o, w_o.astype(BF16)]
        specs += [row, _resident((d, d))]
    args += [p, g1.reshape(1, d), w1.astype(BF16), w2.astype(BF16), g2.reshape(1, d),
             w_gate.astype(BF16), w_proj.astype(BF16)]
    specs += [pl.BlockSpec((None, TM, dp), lambda i: (layer, i, 0)), vec, _resident((d, dff)),
              _resident((dff, d)), vec, _resident((d, d)), _resident((dp, d))]
    if final_g is not None:
        args.append(final_g.reshape(1, d))
        specs.append(vec)
    return pl.pallas_call(
        functools.partial(_ffn_ple_kernel, chunk=1024, with_attn=attn is not None,
                          with_final=final_g is not None),
        grid=(n // TM,),
        in_specs=specs,
        out_specs=row,
        out_shape=jax.ShapeDtypeStruct((n, d), F32),
        compiler_params=_params("parallel"),
        name="ffn_ple",
    )(*args)


def _gate_kernel(h_ref, g_ref, wf_ref, bf_ref, e_ref, ones_ref, c_ref, qa_ref, carry_ref):
    @pl.when(pl.program_id(1) == 0)
    def _():
        carry_ref[...] = jnp.zeros_like(carry_ref)

    tm = h_ref.shape[0]
    xn = _rms(h_ref[...], g_ref[...]).astype(BF16)
    f = jnp.dot(xn, wf_ref[...], preferred_element_type=F32) + bf_ref[...]
    logf = jnp.minimum(f, 0.0) - jnp.log1p(jnp.exp(-jnp.abs(f)))
    rows = lax.broadcasted_iota(jnp.int32, (tm, tm), 0)
    cols = lax.broadcasted_iota(jnp.int32, (tm, tm), 1)
    tri = jnp.where(rows >= cols, 1.0, 0.0).astype(BF16)
    c = sum(jnp.dot(tri, piece, preferred_element_type=F32) for piece in _split3(logf))
    c = c + carry_ref[...]
    c_ref[...] = c
    carry_ref[...] = c[tm - 1:tm, :]
    qa = sum(lax.dot_general(e_ref[r], piece, _NT, preferred_element_type=F32)
             for r, piece in enumerate(_split3(c * LOG2E)))
    qa_ref[...] = (qa + ones_ref[...]).astype(qa_ref.dtype)


def _gates(h, g, w_f, b_f):
    b, s, d = h.shape
    nh = w_f.shape[1]
    wf = jnp.zeros((d, LANES), F32).at[:, :nh].set(w_f).astype(BF16)
    bf = jnp.zeros((1, LANES), F32).at[0, :nh].set(b_f)
    heads = jnp.arange(nh)
    e = jnp.zeros((3, nh * BF16_ROWS, LANES), F32)
    ones = jnp.zeros((nh * BF16_ROWS, 1), F32)
    for r in range(3):
        e = e.at[r, heads * BF16_ROWS + r, heads].set(1.0)
        ones = ones.at[heads * BF16_ROWS + 3 + r, 0].set(1.0)
    return pl.pallas_call(
        _gate_kernel,
        grid=(b, s // TM),
        in_specs=[pl.BlockSpec((None, TM, d), lambda bi, i: (bi, i, 0)), _resident((1, d)),
                  _resident((d, LANES)), _resident((1, LANES)),
                  _resident((3, nh * BF16_ROWS, LANES)), _resident((nh * BF16_ROWS, 1))],
        out_specs=[pl.BlockSpec((None, TM, LANES), lambda bi, i: (bi, i, 0)),
                   pl.BlockSpec((None, nh * BF16_ROWS, TM), lambda bi, i: (bi, 0, i))],
        out_shape=[jax.ShapeDtypeStruct((b, s, LANES), F32),
                   jax.ShapeDtypeStruct((b, nh * BF16_ROWS, s), BF16)],
        scratch_shapes=[pltpu.VMEM((1, LANES), F32)],
        compiler_params=_params("parallel", "arbitrary"),
        name="forget_gate_cumsum",
    )(h, g.reshape(1, d), wf, bf, e.astype(BF16), ones)


def _kv_kernel(h_ref, c_ref, g_ref, wk_ref, wvt_ref, e_ref, ones_ref, k_ref, vt_ref):
    xn = _rms(h_ref[...], g_ref[...]).astype(BF16)
    k = jnp.dot(xn, wk_ref[...], preferred_element_type=F32)
    pieces = jnp.concatenate(_split3(-(c_ref[...] * LOG2E)), axis=1)
    k = k + jnp.dot(pieces, e_ref[...], preferred_element_type=F32) + ones_ref[...]
    k_ref[...] = k.astype(k_ref.dtype)
    vt_ref[...] = lax.dot_general(wvt_ref[...], xn, _NT,
                                  preferred_element_type=F32).astype(vt_ref.dtype)


def _kv(h, c, g, w_k, w_v):
    b, s, d = h.shape
    nh = N_HEADS
    dh = d // nh
    wk = w_k.reshape(d, nh, dh)
    wk = jnp.concatenate([wk, jnp.zeros((d, nh, LANES - dh), wk.dtype)], axis=2)
    heads = jnp.arange(nh)
    e = jnp.zeros((3, LANES, nh * LANES), F32)
    ones = jnp.zeros((1, nh * LANES), F32)
    for r in range(3):
        ones = ones.at[0, heads * LANES + dh + r].set(1.0)
        e = e.at[r, heads, heads * LANES + dh + 3 + r].set(1.0)
    return pl.pallas_call(
        _kv_kernel,
        grid=(b, s // TM),
        in_specs=[pl.BlockSpec((None, TM, d), lambda bi, i: (bi, i, 0)),
                  pl.BlockSpec((None, TM, LANES), lambda bi, i: (bi, i, 0)), _resident((1, d)),
                  _resident((d, nh * LANES)), _resident((d, d)), _resident((3 * LANES, nh * LANES)),
                  _resident((1, nh * LANES))],
        out_specs=[pl.BlockSpec((None, TM, nh * LANES), lambda bi, i: (bi, i, 0)),
                   pl.BlockSpec((None, d, TM), lambda bi, i: (bi, 0, i))],
        out_shape=[jax.ShapeDtypeStruct((b, s, nh * LANES), BF16),
                   jax.ShapeDtypeStruct((b, d, s), BF16)],
        compiler_params=_params("parallel", "parallel"),
        name="kv_proj",
    )(h, c, g.reshape(1, d), wk.reshape(d, nh * LANES).astype(BF16), w_v.T.astype(BF16),
      e.reshape(3 * LANES, nh * LANES).astype(BF16), ones)


def _q_kernel(h_ref, g_ref, wqt_ref, qt_ref, *, scale):
    xn = _rms(h_ref[...], g_ref[...]).astype(BF16)
    qt = lax.dot_general(wqt_ref[...], xn, _NT, preferred_element_type=F32) * scale
    qt_ref[...] = qt.astype(qt_ref.dtype)


def _q(h, g, w_q):
    b, s, d = h.shape
    return pl.pallas_call(
        functools.partial(_q_kernel, scale=(d // N_HEADS) ** -0.5 * LOG2E),
        grid=(b, s // TM),
        in_specs=[pl.BlockSpec((None, TM, d), lambda bi, i: (bi, i, 0)), _resident((1, d)),
                  _resident((d, d))],
        out_specs=pl.BlockSpec((None, d, TM), lambda bi, i: (bi, 0, i)),
        out_shape=jax.ShapeDtypeStruct((b, d, s), BF16),
        compiler_params=_params("parallel", "parallel"),
        name="q_proj",
    )(h, g.reshape(1, d), w_q.T.astype(BF16))


def _attn_kernel(qt_ref, qa_ref, k_ref, vt_ref, o_ref, s_ref, *, dh, nh):
    tq = qt_ref.shape[1]
    tk = s_ref.shape[2]
    nkb = tq // tk
    assert tq == nkb * tk and nkb % 2 == 0
    i = pl.program_id(2)
    kpos = lax.broadcasted_iota(jnp.int32, (tk, tq), 0)
    qpos = lax.broadcasted_iota(jnp.int32, (tk, tq), 1)
    ones = jnp.ones((BF16_ROWS, tk), BF16)
    pad = jnp.zeros((LANES - dh - BF16_ROWS, tq), BF16)

    def scores(hh, j, slot, diag=None):
        kv = pl.ds(pl.multiple_of(j * tk, tk), tk)
        k = k_ref[kv, hh * LANES:(hh + 1) * LANES]
        qt = jnp.concatenate([qt_ref[hh * dh:(hh + 1) * dh, :],
                              qa_ref[hh * BF16_ROWS:(hh + 1) * BF16_ROWS, :], pad], axis=0)
        s = jnp.dot(k, qt, preferred_element_type=F32)
        if diag is not None:
            s = jnp.where(kpos + diag * tk <= qpos, s, NEG_BIG)
        s_ref[slot, hh] = s
        return jnp.max(s, axis=0, keepdims=True)

    def update(hh, j, slot, bmax, carry):
        m, acc = carry
        kv = pl.ds(pl.multiple_of(j * tk, tk), tk)
        vt = jnp.concatenate([vt_ref[hh * dh:(hh + 1) * dh, kv], ones], axis=0)
        m_new = jnp.maximum(m, bmax)
        alpha = jnp.exp2(m - m_new)
        p = jnp.exp2((s_ref[slot, hh] - m_new).astype(BF16))
        acc = alpha * acc + jnp.dot(vt, p, preferred_element_type=F32)
        return m_new, acc

    def half(j, j_next, slot, carry, diag_next=None):
        stats, bmax = carry
        new_stats, new_bmax = [], []
        for hh in range(nh):
            new_bmax.append(scores(hh, j_next, 1 - slot, diag_next))
            new_stats.append(update(hh, j, slot, bmax[hh], stats[hh]))
        return tuple(new_stats), tuple(new_bmax)

    def step(t, carry):
        for d in range(nkb):
            carry = half(nkb * t + d, nkb * t + d + 1, d % 2, carry)
        return carry

    init = (jnp.full((1, tq), NEG_BIG, F32), jnp.zeros((dh + BF16_ROWS, tq), F32))
    carry = (init,) * nh, tuple(scores(hh, nkb * i, 0, 0) for hh in range(nh))
    for d in range(nkb - 1):
        carry = half(nkb * i + d, nkb * i + d + 1, d % 2, carry, d + 1)
    carry = half(nkb * i + nkb - 1, 0, 1, carry)
    stats, _ = lax.fori_loop(0, i, step, carry)
    outs = [acc[:dh] / acc[dh:dh + 1] for (_, acc) in stats]
    o_ref[...] = jnp.concatenate(outs, axis=0).T.astype(o_ref.dtype)


def _attention(qt, qa, k_aug, vt):
    b, d, s = vt.shape
    nh = ATTN_HEADS_PER_STEP
    dh = d // N_HEADS
    return pl.pallas_call(
        functools.partial(_attn_kernel, dh=dh, nh=nh),
        grid=(b, N_HEADS // nh, s // TQ),
        in_specs=[pl.BlockSpec((None, nh * dh, TQ), lambda bi, hg, i: (bi, hg, i)),
                  pl.BlockSpec((None, nh * BF16_ROWS, TQ), lambda bi, hg, i: (bi, hg, i)),
                  pl.BlockSpec((None, s, nh * LANES), lambda bi, hg, i: (bi, 0, hg)),
                  pl.BlockSpec((None, nh * dh, s), lambda bi, hg, i: (bi, hg, 0))],
        out_specs=pl.BlockSpec((None, TQ, nh * dh), lambda bi, hg, i: (bi, i, hg)),
        out_shape=jax.ShapeDtypeStruct((b, s, d), BF16),
        scratch_shapes=[pltpu.VMEM((2, nh, TK, TQ), F32)],
        compiler_params=_params("parallel", "parallel", "arbitrary"),
        name="fox_attention",
    )(qt, qa, k_aug, vt)


def kernel(x, p, mix_norm, conv_w_pw1, conv_b_pw1, conv_w_dw, conv_b_dw, conv_ln_g, conv_ln_b,
           conv_w_pw2, conv_b_pw2, kv_norm, w_kvf, b_f, attn_w_q, attn_w_o, ffn_norm, ffn_w1,
           ffn_w2, ple_norm, ple_w_gate, ple_w_proj, final_norm):
    b, s, d = x.shape
    n = b * s
    depth = mix_norm.shape[0]
    n_a = conv_w_pw1.shape[0]
    assert s % TQ == 0 and s % TM == 0 and s % TM_CONV == 0
    assert d == N_HEADS * (d // N_HEADS) and d // N_HEADS + BF16_ROWS <= LANES

    h = x.reshape(n, d)
    qa = k_aug = vt = None
    for i in range(depth):
        attn = None
        if i < n_a:
            glu = _pw1_glu(h, mix_norm[i], conv_w_pw1[i], conv_b_pw1[i])
            h = _conv_block(glu.reshape(b, s, d), h.reshape(b, s, d), conv_w_dw[i], conv_b_dw[i],
                            conv_ln_g[i], conv_ln_b[i], conv_w_pw2[i], conv_b_pw2[i]).reshape(n, d)
        else:
            j = i - n_a
            h3 = h.reshape(b, s, d)
            if j == 0:
                c, qa = _gates(h3, kv_norm, w_kvf[:, 2 * d:], b_f)
                k_aug, vt = _kv(h3, c, kv_norm, w_kvf[:, :d], w_kvf[:, d:2 * d])
            o = _attention(_q(h3, mix_norm[i], attn_w_q[j]), qa, k_aug, vt)
            attn = (o.reshape(n, d), attn_w_o[j])
        h = _ffn_ple(h, p.reshape(depth, n, -1), i, ffn_norm[i], ffn_w1[i], ffn_w2[i], ple_norm[i],
                     ple_w_gate[i], ple_w_proj[i], attn=attn,
                     final_g=final_norm if i == depth - 1 else None)
    return h.reshape(b, s, d)
```

```python
import functools
import math

import jax
import jax.numpy as jnp
from jax import lax
from jax.experimental import pallas as pl
from jax.experimental.pallas import tpu as pltpu

F32 = jnp.float32
BF16 = jnp.bfloat16

EPS = 1e-6
NEG_BIG = -1e30
N_HEADS = 16
CONV_WIDTH = 31
LOG2E = math.log2(math.e)

LANES = 128
BF16_ROWS = 16
HALO = 32
TM = 512
TM_CONV = 256
TQ = 512
TK = 256
ATTN_HEADS_PER_STEP = 4
VMEM_LIMIT = 56 * 1024 * 1024

_NT = (((1,), (1,)), ((), ()))


def _params(*sem):
    return pltpu.CompilerParams(dimension_semantics=sem, vmem_limit_bytes=VMEM_LIMIT)


def _resident(shape):
    nd = len(shape)
    return pl.BlockSpec(shape, lambda *_: (0,) * nd, pipeline_mode=pl.Buffered(1))


def _rms(x, g):
    return x * lax.rsqrt(jnp.mean(x * x, axis=-1, keepdims=True) + EPS) * g


def _split3(x):
    hi = x.astype(BF16)
    r = x - hi.astype(F32)
    mid = r.astype(BF16)
    lo = (r - mid.astype(F32)).astype(BF16)
    return hi, mid, lo


def _pw1_glu_kernel(h_ref, g_ref, wa_ref, wg_ref, ba_ref, bg_ref, o_ref):
    xn = _rms(h_ref[...], g_ref[...]).astype(BF16)
    a = jnp.dot(xn, wa_ref[...], preferred_element_type=F32) + ba_ref[...]
    gt = jnp.dot(xn, wg_ref[...], preferred_element_type=F32) + bg_ref[...]
    o_ref[...] = (a * jax.nn.sigmoid(gt)).astype(o_ref.dtype)


def _pw1_glu(h, g, w, b):
    n, d = h.shape
    wa, wg = w[:, :d].astype(BF16), w[:, d:].astype(BF16)
    ba, bg = b[:d].reshape(1, d), b[d:].reshape(1, d)
    row = pl.BlockSpec((TM, d), lambda i: (i, 0))
    return pl.pallas_call(
        _pw1_glu_kernel,
        grid=(n // TM,),
        in_specs=[row, _resident((1, d)), _resident((d, d)), _resident((d, d)),
                  _resident((1, d)), _resident((1, d))],
        out_specs=row,
        out_shape=jax.ShapeDtypeStruct((n, d), BF16),
        compiler_params=_params("parallel"),
        name="pw1_glu",
    )(h, g.reshape(1, d), wa, wg, ba, bg)


def _conv_kernel(x_ref, halo_ref, h_ref, wdw_ref, bdw_ref, lg_ref, lb_ref, w2_ref, b2_ref,
                 o_ref, xs_ref, y_ref):
    tm, d = x_ref.shape
    first = pl.program_id(1) == 0
    xs_ref[0:HALO, :] = jnp.where(first, 0.0, halo_ref[...].astype(F32))
    xs_ref[HALO:, :] = x_ref[...].astype(F32)

    groups = [[] for _ in range(8)]
    for k in range(CONV_WIDTH):
        a, b = divmod(HALO - (CONV_WIDTH - 1) + k, 8)
        groups[b].append((a, k))
    row = lax.broadcasted_iota(jnp.int32, (8, LANES), 0)

    def strip(c, carry):
        lanes = pl.ds(pl.multiple_of(c * LANES, LANES), LANES)
        w = [jnp.broadcast_to(wdw_ref[k:k + 1, lanes], (8, LANES)) for k in range(CONV_WIDTH)]
        bias = jnp.broadcast_to(bdw_ref[:, lanes], (8, LANES))

        def u(b, chunk):
            acc = None
            for a, k in groups[b]:
                term = w[k] * xs_ref[(chunk + a) * 8:(chunk + a + 1) * 8, lanes]
                acc = term if acc is None else acc + term
            return acc

        def rotated(chunk):
            return [pltpu.roll(u(b, chunk), 8 - b, 0) for b in range(1, 8)]

        cur = rotated(0)
        for chunk in range(tm // 8):
            nxt = rotated(chunk + 1)
            acc = bias + u(0, chunk)
            for b in range(1, 8):
                acc = acc + jnp.where(row < 8 - b, cur[b - 1], nxt[b - 1])
            y_ref[chunk * 8:(chunk + 1) * 8, lanes] = acc
            cur = nxt
        return carry

    lax.fori_loop(0, d // LANES, strip, 0)

    y = y_ref[...]
    mu = jnp.mean(y, axis=-1, keepdims=True)
    yc = y - mu
    var = jnp.mean(yc * yc, axis=-1, keepdims=True)
    z = yc * lax.rsqrt(var + EPS) * lg_ref[...] + lb_ref[...]
    z = z * jax.nn.sigmoid(z)
    out = jnp.dot(z.astype(BF16), w2_ref[...], preferred_element_type=F32) + b2_ref[...]
    o_ref[...] = h_ref[...] + out


def _conv_block(glu, h, w_dw, b_dw, ln_g, ln_b, w_pw2, b_pw2):
    b, s, d = glu.shape
    tm = TM_CONV
    per = tm // HALO
    wdw = jnp.zeros((HALO, d), F32).at[:CONV_WIDTH].set(w_dw)
    row = pl.BlockSpec((None, tm, d), lambda bi, i: (bi, i, 0))
    halo = pl.BlockSpec((None, HALO, d), lambda bi, i: (bi, jnp.maximum(i * per - 1, 0), 0))
    vec = _resident((1, d))
    return pl.pallas_call(
        _conv_kernel,
        grid=(b, s // tm),
        in_specs=[row, halo, row, _resident((HALO, d)), vec, vec, vec, _resident((d, d)), vec],
        out_specs=row,
        out_shape=jax.ShapeDtypeStruct((b, s, d), F32),
        scratch_shapes=[pltpu.VMEM((tm + HALO, d), F32), pltpu.VMEM((tm, d), F32)],
        compiler_params=_params("parallel", "parallel"),
        name="dwconv_ln_pw2",
    )(glu, glu, h, wdw, b_dw.reshape(1, d), ln_g.reshape(1, d), ln_b.reshape(1, d),
      w_pw2.astype(BF16), b_pw2.reshape(1, d))


def _ffn_ple_kernel(*refs, chunk, with_attn, with_final):
    refs = list(refs)
    h_ref = refs.pop(0)
    h = h_ref[...]
    if with_attn:
        o_ref, wo_ref = refs.pop(0), refs.pop(0)
        h = h + jnp.dot(o_ref[...], wo_ref[...], preferred_element_type=F32)
    p_ref, g1_ref, w1_ref, w2_ref, g2_ref, wg_ref, wp_ref = refs[:7]
    refs = refs[7:]
    xn = _rms(h, g1_ref[...]).astype(BF16)
    for c in range(w1_ref.shape[1] // chunk):
        a = jnp.dot(xn, w1_ref[:, c * chunk:(c + 1) * chunk], preferred_element_type=F32)
        a = jnp.square(jnp.maximum(a, 0.0)).astype(BF16)
        h = h + jnp.dot(a, w2_ref[c * chunk:(c + 1) * chunk, :], preferred_element_type=F32)
    xn = _rms(h, g2_ref[...]).astype(BF16)
    gate = jax.nn.sigmoid(jnp.dot(xn, wg_ref[...], preferred_element_type=F32))
    proj = jnp.dot(p_ref[...].astype(BF16), wp_ref[...], preferred_element_type=F32)
    h = h + gate * proj
    if with_final:
        gf_ref = refs.pop(0)
        h = _rms(h, gf_ref[...])
    out_ref, = refs
    out_ref[...] = h


def _ffn_ple(h, layer, p, g1, w1, w2, g2, w_gate, w_proj, attn=None, final_g=None):
    n, d = h.shape
    dff = w1.shape[2]
    dp = p.shape[2]
    row = pl.BlockSpec((TM, d), lambda i: (i, 0))
    vec = _resident((1, d))

    def of_layer(shape):
        return pl.BlockSpec((None,) + shape, lambda i: (layer, 0, 0), pipeline_mode=pl.Buffered(1))

    args, specs = [h], [row]
    if attn is not None:
        o, w_o = attn
        args += [o, w_o.astype(BF16)]
        specs += [row, _resident((d, d))]
    args += [p, g1.reshape(1, d), w1, w2, g2.reshape(1, d), w_gate, w_proj]
    specs += [pl.BlockSpec((None, TM, dp), lambda i: (layer, i, 0)), vec, of_layer((d, dff)),
              of_layer((dff, d)), vec, of_layer((d, d)), of_layer((dp, d))]
    if final_g is not None:
        args.append(final_g.reshape(1, d))
        specs.append(vec)
    return pl.pallas_call(
        functools.partial(_ffn_ple_kernel, chunk=1024, with_attn=attn is not None,
                          with_final=final_g is not None),
        grid=(n // TM,),
        in_specs=specs,
        out_specs=row,
        out_shape=jax.ShapeDtypeStruct((n, d), F32),
        compiler_params=_params("parallel"),
        name="ffn_ple",
    )(*args)


def _gate_kv_kernel(h_ref, g_ref, wf_ref, bf_ref, eq_ref, onesq_ref, wk_ref, wvt_ref, ek_ref,
                    onesk_ref, qa_ref, k_ref, vt_ref, carry_ref):
    @pl.when(pl.program_id(1) == 0)
    def _():
        carry_ref[...] = jnp.zeros_like(carry_ref)

    tm = h_ref.shape[0]
    xn = _rms(h_ref[...], g_ref[...]).astype(BF16)
    vt_ref[...] = lax.dot_general(wvt_ref[...], xn, _NT,
                                  preferred_element_type=F32).astype(vt_ref.dtype)
    f = jnp.dot(xn, wf_ref[...], preferred_element_type=F32) + bf_ref[...]
    logf = jnp.minimum(f, 0.0) - jnp.log1p(jnp.exp(-jnp.abs(f)))
    rows = lax.broadcasted_iota(jnp.int32, (tm, tm), 0)
    cols = lax.broadcasted_iota(jnp.int32, (tm, tm), 1)
    tri = jnp.where(rows >= cols, 1.0, 0.0).astype(BF16)
    c = sum(jnp.dot(tri, piece, preferred_element_type=F32) for piece in _split3(logf))
    c = c + carry_ref[...]
    carry_ref[...] = c[tm - 1:tm, :]
    qa = sum(lax.dot_general(eq_ref[r], piece, _NT, preferred_element_type=F32)
             for r, piece in enumerate(_split3(c * LOG2E)))
    qa_ref[...] = (qa + onesq_ref[...]).astype(qa_ref.dtype)
    pieces = jnp.concatenate(_split3(-(c * LOG2E)), axis=1)
    k = jnp.dot(xn, wk_ref[...], preferred_element_type=F32)
    k = k + jnp.dot(pieces, ek_ref[...], preferred_element_type=F32) + onesk_ref[...]
    k_ref[...] = k.astype(k_ref.dtype)


def _gate_kv(h, g, w_k, w_v, w_f, b_f):
    b, s, d = h.shape
    nh = N_HEADS
    dh = d // nh
    heads = jnp.arange(nh)
    wf = jnp.zeros((d, LANES), F32).at[:, :nh].set(w_f).astype(BF16)
    bf = jnp.zeros((1, LANES), F32).at[0, :nh].set(b_f)
    eq = jnp.zeros((3, nh * BF16_ROWS, LANES), F32)
    onesq = jnp.zeros((nh * BF16_ROWS, 1), F32)
    ek = jnp.zeros((3, LANES, nh * LANES), F32)
    onesk = jnp.zeros((1, nh * LANES), F32)
    for r in range(3):
        eq = eq.at[r, heads * BF16_ROWS + r, heads].set(1.0)
        onesq = onesq.at[heads * BF16_ROWS + 3 + r, 0].set(1.0)
        onesk = onesk.at[0, heads * LANES + dh + r].set(1.0)
        ek = ek.at[r, heads, heads * LANES + dh + 3 + r].set(1.0)
    wk = w_k.reshape(d, nh, dh)
    wk = jnp.concatenate([wk, jnp.zeros((d, nh, LANES - dh), wk.dtype)], axis=2)
    return pl.pallas_call(
        _gate_kv_kernel,
        grid=(b, s // TM),
        in_specs=[pl.BlockSpec((None, TM, d), lambda bi, i: (bi, i, 0)), _resident((1, d)),
                  _resident((d, LANES)), _resident((1, LANES)),
                  _resident((3, nh * BF16_ROWS, LANES)), _resident((nh * BF16_ROWS, 1)),
                  _resident((d, nh * LANES)), _resident((d, d)),
                  _resident((3 * LANES, nh * LANES)), _resident((1, nh * LANES))],
        out_specs=[pl.BlockSpec((None, nh * BF16_ROWS, TM), lambda bi, i: (bi, 0, i)),
                   pl.BlockSpec((None, TM, nh * LANES), lambda bi, i: (bi, i, 0)),
                   pl.BlockSpec((None, d, TM), lambda bi, i: (bi, 0, i))],
        out_shape=[jax.ShapeDtypeStruct((b, nh * BF16_ROWS, s), BF16),
                   jax.ShapeDtypeStruct((b, s, nh * LANES), BF16),
                   jax.ShapeDtypeStruct((b, d, s), BF16)],
        scratch_shapes=[pltpu.VMEM((1, LANES), F32)],
        compiler_params=_params("parallel", "arbitrary"),
        name="gate_kv_proj",
    )(h, g.reshape(1, d), wf, bf, eq.astype(BF16), onesq, wk.reshape(d, nh * LANES).astype(BF16),
      w_v.T.astype(BF16), ek.reshape(3 * LANES, nh * LANES).astype(BF16), onesk)


def _q_kernel(h_ref, g_ref, wqt_ref, qt_ref, *, scale):
    xn = _rms(h_ref[...], g_ref[...]).astype(BF16)
    qt = lax.dot_general(wqt_ref[...], xn, _NT, preferred_element_type=F32) * scale
    qt_ref[...] = qt.astype(qt_ref.dtype)


def _q(h, g, w_q):
    b, s, d = h.shape
    return pl.pallas_call(
        functools.partial(_q_kernel, scale=(d // N_HEADS) ** -0.5 * LOG2E),
        grid=(b, s // TM),
        in_specs=[pl.BlockSpec((None, TM, d), lambda bi, i: (bi, i, 0)), _resident((1, d)),
                  _resident((d, d))],
        out_specs=pl.BlockSpec((None, d, TM), lambda bi, i: (bi, 0, i)),
        out_shape=jax.ShapeDtypeStruct((b, d, s), BF16),
        compiler_params=_params("parallel", "parallel"),
        name="q_proj",
    )(h, g.reshape(1, d), w_q.T.astype(BF16))


def _attn_kernel(qt_ref, qa_ref, k_ref, vt_ref, o_ref, s_ref, *, dh, nh):
    tq = qt_ref.shape[1]
    tk = s_ref.shape[2]
    nkb = tq // tk
    assert tq == nkb * tk and nkb % 2 == 0
    i = pl.program_id(2)
    kpos = lax.broadcasted_iota(jnp.int32, (tk, tq), 0)
    qpos = lax.broadcasted_iota(jnp.int32, (tk, tq), 1)
    ones = jnp.ones((BF16_ROWS, tk), BF16)
    pad = jnp.zeros((LANES - dh - BF16_ROWS, tq), BF16)

    def scores(hh, j, slot, diag=None):
        kv = pl.ds(pl.multiple_of(j * tk, tk), tk)
        k = k_ref[kv, hh * LANES:(hh + 1) * LANES]
        qt = jnp.concatenate([qt_ref[hh * dh:(hh + 1) * dh, :],
                              qa_ref[hh * BF16_ROWS:(hh + 1) * BF16_ROWS, :], pad], axis=0)
        s = jnp.dot(k, qt, preferred_element_type=F32)
        if diag is not None:
            s = jnp.where(kpos + diag * tk <= qpos, s, NEG_BIG)
        s_ref[slot, hh] = s
        return jnp.max(s, axis=0, keepdims=True)

    def update(hh, j, slot, bmax, carry):
        m, acc = carry
        kv = pl.ds(pl.multiple_of(j * tk, tk), tk)
        vt = jnp.concatenate([vt_ref[hh * dh:(hh + 1) * dh, kv], ones], axis=0)
        m_new = jnp.maximum(m, bmax)
        alpha = jnp.exp2(m - m_new)
        p = jnp.exp2((s_ref[slot, hh] - m_new).astype(BF16))
        acc = alpha * acc + jnp.dot(vt, p, preferred_element_type=F32)
        return m_new, acc

    def half(j, j_next, slot, carry, diag_next=None):
        stats, bmax = carry
        new_stats, new_bmax = [], []
        for hh in range(nh):
            new_bmax.append(scores(hh, j_next, 1 - slot, diag_next))
            new_stats.append(update(hh, j, slot, bmax[hh], stats[hh]))
        return tuple(new_stats), tuple(new_bmax)

    def step(t, carry):
        for d in range(nkb):
            carry = half(nkb * t + d, nkb * t + d + 1, d % 2, carry)
        return carry

    init = (jnp.full((1, tq), NEG_BIG, F32), jnp.zeros((dh + BF16_ROWS, tq), F32))
    carry = (init,) * nh, tuple(scores(hh, nkb * i, 0, 0) for hh in range(nh))
    for d in range(nkb - 1):
        carry = half(nkb * i + d, nkb * i + d + 1, d % 2, carry, d + 1)
    carry = half(nkb * i + nkb - 1, 0, 1, carry)
    stats, _ = lax.fori_loop(0, i, step, carry)
    outs = [acc[:dh] / acc[dh:dh + 1] for (_, acc) in stats]
    o_ref[...] = jnp.concatenate(outs, axis=0).T.astype(o_ref.dtype)


def _attention(qt, qa, k_aug, vt):
    b, d, s = vt.shape
    nh = ATTN_HEADS_PER_STEP
    dh = d // N_HEADS
    return pl.pallas_call(
        functools.partial(_attn_kernel, dh=dh, nh=nh),
        grid=(b, N_HEADS // nh, s // TQ),
        in_specs=[pl.BlockSpec((None, nh * dh, TQ), lambda bi, hg, i: (bi, hg, i)),
                  pl.BlockSpec((None, nh * BF16_ROWS, TQ), lambda bi, hg, i: (bi, hg, i)),
                  pl.BlockSpec((None, s, nh * LANES), lambda bi, hg, i: (bi, 0, hg)),
                  pl.BlockSpec((None, nh * dh, s), lambda bi, hg, i: (bi, hg, 0))],
        out_specs=pl.BlockSpec((None, TQ, nh * dh), lambda bi, hg, i: (bi, i, hg)),
        out_shape=jax.ShapeDtypeStruct((b, s, d), BF16),
        scratch_shapes=[pltpu.VMEM((2, nh, TK, TQ), F32)],
        compiler_params=_params("parallel", "parallel", "arbitrary"),
        name="fox_attention",
    )(qt, qa, k_aug, vt)


def kernel(x, p, mix_norm, conv_w_pw1, conv_b_pw1, conv_w_dw, conv_b_dw, conv_ln_g, conv_ln_b,
           conv_w_pw2, conv_b_pw2, kv_norm, w_kvf, b_f, attn_w_q, attn_w_o, ffn_norm, ffn_w1,
           ffn_w2, ple_norm, ple_w_gate, ple_w_proj, final_norm):
    b, s, d = x.shape
    n = b * s
    depth = mix_norm.shape[0]
    n_a = conv_w_pw1.shape[0]
    assert s % TQ == 0 and s % TM == 0 and s % TM_CONV == 0
    assert d == N_HEADS * (d // N_HEADS) and d // N_HEADS + BF16_ROWS <= LANES

    h = x.reshape(n, d)
    p = p.reshape(depth, n, -1)
    ffn_w1, ffn_w2 = ffn_w1.astype(BF16), ffn_w2.astype(BF16)
    ple_w_gate, ple_w_proj = ple_w_gate.astype(BF16), ple_w_proj.astype(BF16)
    qa = k_aug = vt = None
    for i in range(depth):
        attn = None
        if i < n_a:
            glu = _pw1_glu(h, mix_norm[i], conv_w_pw1[i], conv_b_pw1[i])
            h = _conv_block(glu.reshape(b, s, d), h.reshape(b, s, d), conv_w_dw[i], conv_b_dw[i],
                            conv_ln_g[i], conv_ln_b[i], conv_w_pw2[i], conv_b_pw2[i]).reshape(n, d)
        else:
            j = i - n_a
            h3 = h.reshape(b, s, d)
            if j == 0:
                qa, k_aug, vt = _gate_kv(h3, kv_norm, w_kvf[:, :d], w_kvf[:, d:2 * d],
                                         w_kvf[:, 2 * d:], b_f)
            o = _attention(_q(h3, mix_norm[i], attn_w_q[j]), qa, k_aug, vt)
            attn = (o.reshape(n, d), attn_w_o[j])
        h = _ffn_ple(h, i, p, ffn_norm[i], ffn_w1, ffn_w2, ple_norm[i], ple_w_gate, ple_w_proj,
                     attn=attn, final_g=final_norm if i == depth - 1 else None)
    return h.reshape(b, s, d)
```

```python
import functools
import math

import jax
import jax.numpy as jnp
from jax import lax
from jax.experimental import pallas as pl
from jax.experimental.pallas import tpu as pltpu

F32 = jnp.float32
BF16 = jnp.bfloat16

EPS = 1e-6
NEG_BIG = -1e30
N_HEADS = 16
CONV_WIDTH = 31
LOG2E = math.log2(math.e)

LANES = 128
BF16_ROWS = 16
HALO = 32
TM = 512
TM_CONV = 512
TQ = 512
TK = 256
ATTN_HEADS_PER_STEP = 4
VMEM_LIMIT = 56 * 1024 * 1024

_NT = (((1,), (1,)), ((), ()))


def _params(*sem):
    return pltpu.CompilerParams(dimension_semantics=sem, vmem_limit_bytes=VMEM_LIMIT)


def _resident(shape):
    nd = len(shape)
    return pl.BlockSpec(shape, lambda *_: (0,) * nd, pipeline_mode=pl.Buffered(1))


def _rms(x, g):
    return x * lax.rsqrt(jnp.mean(x * x, axis=-1, keepdims=True) + EPS) * g


def _split3(x):
    hi = x.astype(BF16)
    r = x - hi.astype(F32)
    mid = r.astype(BF16)
    lo = (r - mid.astype(F32)).astype(BF16)
    return hi, mid, lo


def _glu(xn, wa_ref, wg_ref, ba_ref, bg_ref):
    a = jnp.dot(xn, wa_ref[...], preferred_element_type=F32) + ba_ref[...]
    gt = jnp.dot(xn, wg_ref[...], preferred_element_type=F32) + bg_ref[...]
    return a * jax.nn.sigmoid(gt)


def _pw1_glu_kernel(h_ref, g_ref, wa_ref, wg_ref, ba_ref, bg_ref, o_ref):
    xn = _rms(h_ref[...], g_ref[...]).astype(BF16)
    o_ref[...] = _glu(xn, wa_ref, wg_ref, ba_ref, bg_ref).astype(o_ref.dtype)


def _pw1_glu(h, g, w, b):
    n, d = h.shape
    wa, wg = w[:, :d].astype(BF16), w[:, d:].astype(BF16)
    ba, bg = b[:d].reshape(1, d), b[d:].reshape(1, d)
    row = pl.BlockSpec((TM, d), lambda i: (i, 0))
    return pl.pallas_call(
        _pw1_glu_kernel,
        grid=(n // TM,),
        in_specs=[row, _resident((1, d)), _resident((d, d)), _resident((d, d)),
                  _resident((1, d)), _resident((1, d))],
        out_specs=row,
        out_shape=jax.ShapeDtypeStruct((n, d), BF16),
        compiler_params=_params("parallel"),
        name="pw1_glu",
    )(h, g.reshape(1, d), wa, wg, ba, bg)


def _conv_kernel(x_ref, halo_ref, h_ref, wdw_ref, bdw_ref, lg_ref, lb_ref, w2_ref, b2_ref,
                 o_ref, xs_ref, y_ref):
    tm, d = x_ref.shape
    first = pl.program_id(1) == 0
    xs_ref[0:HALO, :] = jnp.where(first, 0.0, halo_ref[...].astype(F32))
    xs_ref[HALO:, :] = x_ref[...].astype(F32)

    groups = [[] for _ in range(8)]
    for k in range(CONV_WIDTH):
        a, b = divmod(HALO - (CONV_WIDTH - 1) + k, 8)
        groups[b].append((a, k))
    row = lax.broadcasted_iota(jnp.int32, (8, LANES), 0)

    def strip(c, carry):
        lanes = pl.ds(pl.multiple_of(c * LANES, LANES), LANES)
        w = [jnp.broadcast_to(wdw_ref[k:k + 1, lanes], (8, LANES)) for k in range(CONV_WIDTH)]
        bias = jnp.broadcast_to(bdw_ref[:, lanes], (8, LANES))

        def u(b, chunk):
            acc = None
            for a, k in groups[b]:
                term = w[k] * xs_ref[(chunk + a) * 8:(chunk + a + 1) * 8, lanes]
                acc = term if acc is None else acc + term
            return acc

        def rotated(chunk):
            return [pltpu.roll(u(b, chunk), 8 - b, 0) for b in range(1, 8)]

        cur = rotated(0)
        for chunk in range(tm // 8):
            nxt = rotated(chunk + 1)
            acc = bias + u(0, chunk)
            for b in range(1, 8):
                acc = acc + jnp.where(row < 8 - b, cur[b - 1], nxt[b - 1])
            y_ref[chunk * 8:(chunk + 1) * 8, lanes] = acc
            cur = nxt
        return carry

    lax.fori_loop(0, d // LANES, strip, 0)

    y = y_ref[...]
    mu = jnp.mean(y, axis=-1, keepdims=True)
    yc = y - mu
    var = jnp.mean(yc * yc, axis=-1, keepdims=True)
    z = yc * lax.rsqrt(var + EPS) * lg_ref[...] + lb_ref[...]
    z = z * jax.nn.sigmoid(z)
    out = jnp.dot(z.astype(BF16), w2_ref[...], preferred_element_type=F32) + b2_ref[...]
    o_ref[...] = h_ref[...] + out


def _conv_block(glu, h, w_dw, b_dw, ln_g, ln_b, w_pw2, b_pw2):
    b, s, d = glu.shape
    tm = TM_CONV
    per = tm // HALO
    wdw = jnp.zeros((HALO, d), F32).at[:CONV_WIDTH].set(w_dw)
    row = pl.BlockSpec((None, tm, d), lambda bi, i: (bi, i, 0))
    halo = pl.BlockSpec((None, HALO, d), lambda bi, i: (bi, jnp.maximum(i * per - 1, 0), 0))
    vec = _resident((1, d))
    return pl.pallas_call(
        _conv_kernel,
        grid=(b, s // tm),
        in_specs=[row, halo, row, _resident((HALO, d)), vec, vec, vec, _resident((d, d)), vec],
        out_specs=row,
        out_shape=jax.ShapeDtypeStruct((b, s, d), F32),
        scratch_shapes=[pltpu.VMEM((tm + HALO, d), F32), pltpu.VMEM((tm, d), F32)],
        compiler_params=_params("parallel", "parallel"),
        name="dwconv_ln_pw2",
    )(glu, glu, h, wdw, b_dw.reshape(1, d), ln_g.reshape(1, d), ln_b.reshape(1, d),
      w_pw2.astype(BF16), b_pw2.reshape(1, d))


def _ffn_ple_kernel(*refs, chunk, with_attn, tail, q_scale):
    refs = list(refs)
    h_ref = refs.pop(0)
    h = h_ref[...]
    if with_attn:
        o_ref, wo_ref = refs.pop(0), refs.pop(0)
        h = h + jnp.dot(o_ref[...], wo_ref[...], preferred_element_type=F32)
    p_ref, g1_ref, w1_ref, w2_ref, g2_ref, wg_ref, wp_ref = refs[:7]
    refs = refs[7:]
    xn = _rms(h, g1_ref[...]).astype(BF16)
    for c in range(w1_ref.shape[1] // chunk):
        a = jnp.dot(xn, w1_ref[:, c * chunk:(c + 1) * chunk], preferred_element_type=F32)
        a = jnp.square(jnp.maximum(a, 0.0)).astype(BF16)
        h = h + jnp.dot(a, w2_ref[c * chunk:(c + 1) * chunk, :], preferred_element_type=F32)
    xn = _rms(h, g2_ref[...]).astype(BF16)
    gate = jax.nn.sigmoid(jnp.dot(xn, wg_ref[...], preferred_element_type=F32))
    proj = jnp.dot(p_ref[...].astype(BF16), wp_ref[...], preferred_element_type=F32)
    h = h + gate * proj
    if tail == "final":
        gf_ref, out_ref = refs
        out_ref[...] = _rms(h, gf_ref[...])
    elif tail == "glu":
        g3_ref, wa_ref, wgl_ref, ba_ref, bg_ref, out_ref, glu_ref = refs
        out_ref[...] = h
        xn = _rms(h, g3_ref[...]).astype(BF16)
        glu_ref[...] = _glu(xn, wa_ref, wgl_ref, ba_ref, bg_ref).astype(glu_ref.dtype)
    else:
        g3_ref, wqt_ref, out_ref, qt_ref = refs
        out_ref[...] = h
        xn = _rms(h, g3_ref[...]).astype(BF16)
        qt = lax.dot_general(wqt_ref[...], xn, _NT, preferred_element_type=F32) * q_scale
        qt_ref[...] = qt.astype(qt_ref.dtype)


def _ffn_ple(h, layer, p, g1, w1, w2, g2, w_gate, w_proj, attn=None, final_g=None, next_glu=None,
             next_q=None, seq=None):
    n, d = h.shape
    dff = w1.shape[2]
    dp = p.shape[2]
    row = pl.BlockSpec((TM, d), lambda i: (i, 0))
    vec = _resident((1, d))

    def of_layer(shape):
        return pl.BlockSpec((None,) + shape, lambda i: (layer, 0, 0), pipeline_mode=pl.Buffered(1))

    args, specs = [h], [row]
    if attn is not None:
        o, w_o = attn
        args += [o, w_o.astype(BF16)]
        specs += [row, _resident((d, d))]
    args += [p, g1.reshape(1, d), w1, w2, g2.reshape(1, d), w_gate, w_proj]
    specs += [pl.BlockSpec((None, TM, dp), lambda i: (layer, i, 0)), vec, of_layer((d, dff)),
              of_layer((dff, d)), vec, of_layer((d, d)), of_layer((dp, d))]
    out_specs, out_shape = [row], [jax.ShapeDtypeStruct((n, d), F32)]
    if final_g is not None:
        tail = "final"
        args.append(final_g.reshape(1, d))
        specs.append(vec)
    elif next_glu is not None:
        tail = "glu"
        g3, w, bias = next_glu
        args += [g3.reshape(1, d), w[:, :d].astype(BF16), w[:, d:].astype(BF16),
                 bias[:d].reshape(1, d), bias[d:].reshape(1, d)]
        specs += [vec, _resident((d, d)), _resident((d, d)), vec, vec]
        out_specs.append(row)
        out_shape.append(jax.ShapeDtypeStruct((n, d), BF16))
    else:
        tail = "q"
        g3, w_q = next_q
        per_seq = seq // TM
        args += [g3.reshape(1, d), w_q.T.astype(BF16)]
        specs += [vec, _resident((d, d))]
        out_specs.append(pl.BlockSpec((None, d, TM), lambda i: (i // per_seq, 0, i % per_seq)))
        out_shape.append(jax.ShapeDtypeStruct((n // seq, d, seq), BF16))
    out = pl.pallas_call(
        functools.partial(_ffn_ple_kernel, chunk=1024, with_attn=attn is not None, tail=tail,
                          q_scale=(d // N_HEADS) ** -0.5 * LOG2E),
        grid=(n // TM,),
        in_specs=specs,
        out_specs=out_specs,
        out_shape=out_shape,
        compiler_params=_params("parallel"),
        name="ffn_ple",
    )(*args)
    return out[0] if tail == "final" else out


def _gate_kv_kernel(h_ref, g_ref, wf_ref, bf_ref, eq_ref, onesq_ref, wk_ref, wvt_ref, ek_ref,
                    onesk_ref, qa_ref, k_ref, vt_ref, carry_ref):
    @pl.when(pl.program_id(1) == 0)
    def _():
        carry_ref[...] = jnp.zeros_like(carry_ref)

    tm = h_ref.shape[0]
    xn = _rms(h_ref[...], g_ref[...]).astype(BF16)
    vt_ref[...] = lax.dot_general(wvt_ref[...], xn, _NT,
                                  preferred_element_type=F32).astype(vt_ref.dtype)
    f = jnp.dot(xn, wf_ref[...], preferred_element_type=F32) + bf_ref[...]
    logf = jnp.minimum(f, 0.0) - jnp.log1p(jnp.exp(-jnp.abs(f)))
    rows = lax.broadcasted_iota(jnp.int32, (tm, tm), 0)
    cols = lax.broadcasted_iota(jnp.int32, (tm, tm), 1)
    tri = jnp.where(rows >= cols, 1.0, 0.0).astype(BF16)
    c = sum(jnp.dot(tri, piece, preferred_element_type=F32) for piece in _split3(logf))
    c = c + carry_ref[...]
    carry_ref[...] = c[tm - 1:tm, :]
    qa = sum(lax.dot_general(eq_ref[r], piece, _NT, preferred_element_type=F32)
             for r, piece in enumerate(_split3(c * LOG2E)))
    qa_ref[...] = (qa + onesq_ref[...]).astype(qa_ref.dtype)
    pieces = jnp.concatenate(_split3(-(c * LOG2E)), axis=1)
    k = jnp.dot(xn, wk_ref[...], preferred_element_type=F32)
    k = k + jnp.dot(pieces, ek_ref[...], preferred_element_type=F32) + onesk_ref[...]
    k_ref[...] = k.astype(k_ref.dtype)


def _gate_kv(h, g, w_k, w_v, w_f, b_f):
    b, s, d = h.shape
    nh = N_HEADS
    dh = d // nh
    heads = jnp.arange(nh)
    wf = jnp.zeros((d, LANES), F32).at[:, :nh].set(w_f).astype(BF16)
    bf = jnp.zeros((1, LANES), F32).at[0, :nh].set(b_f)
    eq = jnp.zeros((3, nh * BF16_ROWS, LANES), F32)
    onesq = jnp.zeros((nh * BF16_ROWS, 1), F32)
    ek = jnp.zeros((3, LANES, nh * LANES), F32)
    onesk = jnp.zeros((1, nh * LANES), F32)
    for r in range(3):
        eq = eq.at[r, heads * BF16_ROWS + r, heads].set(1.0)
        onesq = onesq.at[heads * BF16_ROWS + 3 + r, 0].set(1.0)
        onesk = onesk.at[0, heads * LANES + dh + r].set(1.0)
        ek = ek.at[r, heads, heads * LANES + dh + 3 + r].set(1.0)
    wk = w_k.reshape(d, nh, dh)
    wk = jnp.concatenate([wk, jnp.zeros((d, nh, LANES - dh), wk.dtype)], axis=2)
    return pl.pallas_call(
        _gate_kv_kernel,
        grid=(b, s // TM),
        in_specs=[pl.BlockSpec((None, TM, d), lambda bi, i: (bi, i, 0)), _resident((1, d)),
                  _resident((d, LANES)), _resident((1, LANES)),
                  _resident((3, nh * BF16_ROWS, LANES)), _resident((nh * BF16_ROWS, 1)),
                  _resident((d, nh * LANES)), _resident((d, d)),
                  _resident((3 * LANES, nh * LANES)), _resident((1, nh * LANES))],
        out_specs=[pl.BlockSpec((None, nh * BF16_ROWS, TM), lambda bi, i: (bi, 0, i)),
                   pl.BlockSpec((None, TM, nh * LANES), lambda bi, i: (bi, i, 0)),
                   pl.BlockSpec((None, d, TM), lambda bi, i: (bi, 0, i))],
        out_shape=[jax.ShapeDtypeStruct((b, nh * BF16_ROWS, s), BF16),
                   jax.ShapeDtypeStruct((b, s, nh * LANES), BF16),
                   jax.ShapeDtypeStruct((b, d, s), BF16)],
        scratch_shapes=[pltpu.VMEM((1, LANES), F32)],
        compiler_params=_params("parallel", "arbitrary"),
        name="gate_kv_proj",
    )(h, g.reshape(1, d), wf, bf, eq.astype(BF16), onesq, wk.reshape(d, nh * LANES).astype(BF16),
      w_v.T.astype(BF16), ek.reshape(3 * LANES, nh * LANES).astype(BF16), onesk)


def _attn_kernel(qt_ref, qa_ref, k_ref, vt_ref, o_ref, s_ref, *, dh, nh):
    tq = qt_ref.shape[1]
    tk = s_ref.shape[2]
    nkb = tq // tk
    assert tq == nkb * tk and nkb % 2 == 0
    i = pl.program_id(2)
    kpos = lax.broadcasted_iota(jnp.int32, (tk, tq), 0)
    qpos = lax.broadcasted_iota(jnp.int32, (tk, tq), 1)
    ones = jnp.ones((BF16_ROWS, tk), BF16)
    pad = jnp.zeros((LANES - dh - BF16_ROWS, tq), BF16)

    def scores(hh, j, slot, diag=None):
        kv = pl.ds(pl.multiple_of(j * tk, tk), tk)
        k = k_ref[kv, hh * LANES:(hh + 1) * LANES]
        qt = jnp.concatenate([qt_ref[hh * dh:(hh + 1) * dh, :],
                              qa_ref[hh * BF16_ROWS:(hh + 1) * BF16_ROWS, :], pad], axis=0)
        s = jnp.dot(k, qt, preferred_element_type=F32)
        if diag is not None:
            s = jnp.where(kpos + diag * tk <= qpos, s, NEG_BIG)
        s_ref[slot, hh] = s
        return jnp.max(s, axis=0, keepdims=True)

    def update(hh, j, slot, bmax, carry):
        m, acc = carry
        kv = pl.ds(pl.multiple_of(j * tk, tk), tk)
        vt = jnp.concatenate([vt_ref[hh * dh:(hh + 1) * dh, kv], ones], axis=0)
        m_new = jnp.maximum(m, bmax)
        alpha = jnp.exp2(m - m_new)
        p = jnp.exp2((s_ref[slot, hh] - m_new).astype(BF16))
        acc = alpha * acc + jnp.dot(vt, p, preferred_element_type=F32)
        return m_new, acc

    def half(j, j_next, slot, carry, diag_next=None):
        stats, bmax = carry
        new_stats, new_bmax = [], []
        for hh in range(nh):
            new_bmax.append(scores(hh, j_next, 1 - slot, diag_next))
            new_stats.append(update(hh, j, slot, bmax[hh], stats[hh]))
        return tuple(new_stats), tuple(new_bmax)

    def step(t, carry):
        for d in range(nkb):
            carry = half(nkb * t + d, nkb * t + d + 1, d % 2, carry)
        return carry

    init = (jnp.full((1, tq), NEG_BIG, F32), jnp.zeros((dh + BF16_ROWS, tq), F32))
    carry = (init,) * nh, tuple(scores(hh, nkb * i, 0, 0) for hh in range(nh))
    for d in range(nkb - 1):
        carry = half(nkb * i + d, nkb * i + d + 1, d % 2, carry, d + 1)
    carry = half(nkb * i + nkb - 1, 0, 1, carry)
    stats, _ = lax.fori_loop(0, i, step, carry)
    outs = [acc[:dh] / acc[dh:dh + 1] for (_, acc) in stats]
    o_ref[...] = jnp.concatenate(outs, axis=0).T.astype(o_ref.dtype)


def _attention(qt, qa, k_aug, vt):
    b, d, s = vt.shape
    nh = ATTN_HEADS_PER_STEP
    dh = d // N_HEADS
    return pl.pallas_call(
        functools.partial(_attn_kernel, dh=dh, nh=nh),
        grid=(b, N_HEADS // nh, s // TQ),
        in_specs=[pl.BlockSpec((None, nh * dh, TQ), lambda bi, hg, i: (bi, hg, i)),
                  pl.BlockSpec((None, nh * BF16_ROWS, TQ), lambda bi, hg, i: (bi, hg, i)),
                  pl.BlockSpec((None, s, nh * LANES), lambda bi, hg, i: (bi, 0, hg)),
                  pl.BlockSpec((None, nh * dh, s), lambda bi, hg, i: (bi, hg, 0))],
        out_specs=pl.BlockSpec((None, TQ, nh * dh), lambda bi, hg, i: (bi, i, hg)),
        out_shape=jax.ShapeDtypeStruct((b, s, d), BF16),
        scratch_shapes=[pltpu.VMEM((2, nh, TK, TQ), F32)],
        compiler_params=_params("parallel", "parallel", "arbitrary"),
        name="fox_attention",
    )(qt, qa, k_aug, vt)


def kernel(x, p, mix_norm, conv_w_pw1, conv_b_pw1, conv_w_dw, conv_b_dw, conv_ln_g, conv_ln_b,
           conv_w_pw2, conv_b_pw2, kv_norm, w_kvf, b_f, attn_w_q, attn_w_o, ffn_norm, ffn_w1,
           ffn_w2, ple_norm, ple_w_gate, ple_w_proj, final_norm):
    b, s, d = x.shape
    n = b * s
    depth = mix_norm.shape[0]
    n_a = conv_w_pw1.shape[0]
    assert s % TQ == 0 and s % TM == 0 and s % TM_CONV == 0
    assert d == N_HEADS * (d // N_HEADS) and d // N_HEADS + BF16_ROWS <= LANES

    h = x.reshape(n, d)
    p = p.reshape(depth, n, -1)
    ffn_w1, ffn_w2 = ffn_w1.astype(BF16), ffn_w2.astype(BF16)
    ple_w_gate, ple_w_proj = ple_w_gate.astype(BF16), ple_w_proj.astype(BF16)
    assert 1 <= n_a < depth
    qa = k_aug = vt = None
    glu = _pw1_glu(h, mix_norm[0], conv_w_pw1[0], conv_b_pw1[0])
    qt = None
    for i in range(depth):
        attn = None
        if i < n_a:
            h = _conv_block(glu.reshape(b, s, d), h.reshape(b, s, d), conv_w_dw[i], conv_b_dw[i],
                            conv_ln_g[i], conv_ln_b[i], conv_w_pw2[i], conv_b_pw2[i]).reshape(n, d)
        else:
            j = i - n_a
            if j == 0:
                qa, k_aug, vt = _gate_kv(h.reshape(b, s, d), kv_norm, w_kvf[:, :d],
                                         w_kvf[:, d:2 * d], w_kvf[:, 2 * d:], b_f)
            attn = (_attention(qt, qa, k_aug, vt).reshape(n, d), attn_w_o[j])
        tail = {}
        if i == depth - 1:
            tail["final_g"] = final_norm
        elif i + 1 < n_a:
            tail["next_glu"] = (mix_norm[i + 1], conv_w_pw1[i + 1], conv_b_pw1[i + 1])
        else:
            tail["next_q"] = (mix_norm[i + 1], attn_w_q[i + 1 - n_a])
        out = _ffn_ple(h, i, p, ffn_norm[i], ffn_w1, ffn_w2, ple_norm[i], ple_w_gate, ple_w_proj,
                       attn=attn, seq=s, **tail)
        if i == depth - 1:
            h = out
        elif i + 1 < n_a:
            h, glu = out
        else:
            h, qt = out
    return h.reshape(b, s, d)
```

```python
import functools
import math

import jax
import jax.numpy as jnp
from jax import lax
from jax.experimental import pallas as pl
from jax.experimental.pallas import tpu as pltpu

F32 = jnp.float32
BF16 = jnp.bfloat16

EPS = 1e-6
NEG_BIG = -1e30
N_HEADS = 16
CONV_WIDTH = 31
LOG2E = math.log2(math.e)

LANES = 128
BF16_ROWS = 16
HALO = 32
TM = 512
TM_FFN = 512
TM_CONV = 512
TQ = 512
TK = 256
ATTN_HEADS_PER_STEP = 4
VMEM_LIMIT = 56 * 1024 * 1024

_NT = (((1,), (1,)), ((), ()))


def _params(*sem):
    return pltpu.CompilerParams(dimension_semantics=sem, vmem_limit_bytes=VMEM_LIMIT)


def _resident(shape):
    nd = len(shape)
    return pl.BlockSpec(shape, lambda *_: (0,) * nd, pipeline_mode=pl.Buffered(1))


def _rms(x, g):
    return x * lax.rsqrt(jnp.mean(x * x, axis=-1, keepdims=True) + EPS) * g


def _split3(x):
    hi = x.astype(BF16)
    r = x - hi.astype(F32)
    mid = r.astype(BF16)
    lo = (r - mid.astype(F32)).astype(BF16)
    return hi, mid, lo


def _glu(xn, wa_ref, wg_ref, ba_ref, bg_ref):
    a = jnp.dot(xn, wa_ref[...], preferred_element_type=F32) + ba_ref[...]
    gt = jnp.dot(xn, wg_ref[...], preferred_element_type=F32) + bg_ref[...]
    return a * jax.nn.sigmoid(gt)


def _pw1_glu_kernel(h_ref, g_ref, wa_ref, wg_ref, ba_ref, bg_ref, o_ref):
    xn = _rms(h_ref[...], g_ref[...]).astype(BF16)
    o_ref[...] = _glu(xn, wa_ref, wg_ref, ba_ref, bg_ref).astype(o_ref.dtype)


def _pw1_glu(h, g, w, b):
    n, d = h.shape
    wa, wg = w[:, :d].astype(BF16), w[:, d:].astype(BF16)
    ba, bg = b[:d].reshape(1, d), b[d:].reshape(1, d)
    row = pl.BlockSpec((TM, d), lambda i: (i, 0))
    return pl.pallas_call(
        _pw1_glu_kernel,
        grid=(n // TM,),
        in_specs=[row, _resident((1, d)), _resident((d, d)), _resident((d, d)),
                  _resident((1, d)), _resident((1, d))],
        out_specs=row,
        out_shape=jax.ShapeDtypeStruct((n, d), BF16),
        compiler_params=_params("parallel"),
        name="pw1_glu",
    )(h, g.reshape(1, d), wa, wg, ba, bg)


def _conv_kernel(x_ref, halo_ref, h_ref, wdw_ref, bdw_ref, lg_ref, lb_ref, w2_ref, b2_ref,
                 o_ref, xs_ref, y_ref):
    tm, d = x_ref.shape
    first = pl.program_id(1) == 0
    xs_ref[0:HALO, :] = jnp.where(first, 0.0, halo_ref[...].astype(F32))
    xs_ref[HALO:, :] = x_ref[...].astype(F32)

    groups = [[] for _ in range(8)]
    for k in range(CONV_WIDTH):
        a, b = divmod(HALO - (CONV_WIDTH - 1) + k, 8)
        groups[b].append((a, k))
    row = lax.broadcasted_iota(jnp.int32, (8, LANES), 0)

    def strip(c, carry):
        lanes = pl.ds(pl.multiple_of(c * LANES, LANES), LANES)
        w = [jnp.broadcast_to(wdw_ref[k:k + 1, lanes], (8, LANES)) for k in range(CONV_WIDTH)]
        bias = jnp.broadcast_to(bdw_ref[:, lanes], (8, LANES))

        def u(b, chunk):
            acc = None
            for a, k in groups[b]:
                term = w[k] * xs_ref[(chunk + a) * 8:(chunk + a + 1) * 8, lanes]
                acc = term if acc is None else acc + term
            return acc

        def rotated(chunk):
            return [pltpu.roll(u(b, chunk), 8 - b, 0) for b in range(1, 8)]

        cur = rotated(0)
        for chunk in range(tm // 8):
            nxt = rotated(chunk + 1)
            acc = bias + u(0, chunk)
            for b in range(1, 8):
                acc = acc + jnp.where(row < 8 - b, cur[b - 1], nxt[b - 1])
            y_ref[chunk * 8:(chunk + 1) * 8, lanes] = acc
            cur = nxt
        return carry

    lax.fori_loop(0, d // LANES, strip, 0)

    y = y_ref[...]
    mu = jnp.mean(y, axis=-1, keepdims=True)
    yc = y - mu
    var = jnp.mean(yc * yc, axis=-1, keepdims=True)
    z = yc * lax.rsqrt(var + EPS) * lg_ref[...] + lb_ref[...]
    z = z * jax.nn.sigmoid(z)
    out = jnp.dot(z.astype(BF16), w2_ref[...], preferred_element_type=F32) + b2_ref[...]
    o_ref[...] = h_ref[...] + out


def _conv_block(glu, h, w_dw, b_dw, ln_g, ln_b, w_pw2, b_pw2):
    b, s, d = glu.shape
    tm = TM_CONV
    per = tm // HALO
    wdw = jnp.zeros((HALO, d), F32).at[:CONV_WIDTH].set(w_dw)
    row = pl.BlockSpec((None, tm, d), lambda bi, i: (bi, i, 0))
    halo = pl.BlockSpec((None, HALO, d), lambda bi, i: (bi, jnp.maximum(i * per - 1, 0), 0))
    vec = _resident((1, d))
    return pl.pallas_call(
        _conv_kernel,
        grid=(b, s // tm),
        in_specs=[row, halo, row, _resident((HALO, d)), vec, vec, vec, _resident((d, d)), vec],
        out_specs=row,
        out_shape=jax.ShapeDtypeStruct((b, s, d), F32),
        scratch_shapes=[pltpu.VMEM((tm + HALO, d), F32), pltpu.VMEM((tm, d), F32)],
        compiler_params=_params("parallel", "parallel"),
        name="dwconv_ln_pw2",
    )(glu, glu, h, wdw, b_dw.reshape(1, d), ln_g.reshape(1, d), ln_b.reshape(1, d),
      w_pw2.astype(BF16), b_pw2.reshape(1, d))


def _ffn_ple_kernel(*refs, chunk, with_attn, tail, q_scale):
    refs = list(refs)
    h_ref = refs.pop(0)
    h = h_ref[...]
    if with_attn:
        o_ref, wo_ref = refs.pop(0), refs.pop(0)
        h = h + jnp.dot(o_ref[...], wo_ref[...], preferred_element_type=F32)
    p_ref, g1_ref, w1_ref, w2_ref, g2_ref, wg_ref, wp_ref = refs[:7]
    refs = refs[7:]
    xn = _rms(h, g1_ref[...]).astype(BF16)
    for c in range(w1_ref.shape[1] // chunk):
        a = jnp.dot(xn, w1_ref[:, c * chunk:(c + 1) * chunk], preferred_element_type=F32)
        a = jnp.square(jnp.maximum(a, 0.0)).astype(BF16)
        h = h + jnp.dot(a, w2_ref[c * chunk:(c + 1) * chunk, :], preferred_element_type=F32)
    xn = _rms(h, g2_ref[...]).astype(BF16)
    gate = jax.nn.sigmoid(jnp.dot(xn, wg_ref[...], preferred_element_type=F32))
    proj = jnp.dot(p_ref[...].astype(BF16), wp_ref[...], preferred_element_type=F32)
    h = h + gate * proj
    if tail == "final":
        gf_ref, out_ref = refs
        out_ref[...] = _rms(h, gf_ref[...])
    elif tail == "glu":
        g3_ref, wa_ref, wgl_ref, ba_ref, bg_ref, out_ref, glu_ref = refs
        out_ref[...] = h
        xn = _rms(h, g3_ref[...]).astype(BF16)
        glu_ref[...] = _glu(xn, wa_ref, wgl_ref, ba_ref, bg_ref).astype(glu_ref.dtype)
    else:
        g3_ref, wqt_ref, out_ref, qt_ref = refs
        out_ref[...] = h
        xn = _rms(h, g3_ref[...]).astype(BF16)
        qt = lax.dot_general(wqt_ref[...], xn, _NT, preferred_element_type=F32) * q_scale
        qt_ref[...] = qt.astype(qt_ref.dtype)


def _ffn_ple(h, layer, p, g1, w1, w2, g2, w_gate, w_proj, attn=None, final_g=None, next_glu=None,
             next_q=None, seq=None):
    n, d = h.shape
    dff = w1.shape[2]
    dp = p.shape[2]
    tm = TM_FFN
    row = pl.BlockSpec((tm, d), lambda i: (i, 0))
    vec = _resident((1, d))

    def of_layer(shape):
        return pl.BlockSpec((None,) + shape, lambda i: (layer, 0, 0), pipeline_mode=pl.Buffered(1))

    args, specs = [h], [row]
    if attn is not None:
        o, w_o = attn
        args += [o, w_o.astype(BF16)]
        specs += [row, _resident((d, d))]
    args += [p, g1.reshape(1, d), w1, w2, g2.reshape(1, d), w_gate, w_proj]
    specs += [pl.BlockSpec((None, tm, dp), lambda i: (layer, i, 0)), vec, of_layer((d, dff)),
              of_layer((dff, d)), vec, of_layer((d, d)), of_layer((dp, d))]
    out_specs, out_shape = [row], [jax.ShapeDtypeStruct((n, d), F32)]
    if final_g is not None:
        tail = "final"
        args.append(final_g.reshape(1, d))
        specs.append(vec)
    elif next_glu is not None:
        tail = "glu"
        g3, w, bias = next_glu
        args += [g3.reshape(1, d), w[:, :d].astype(BF16), w[:, d:].astype(BF16),
                 bias[:d].reshape(1, d), bias[d:].reshape(1, d)]
        specs += [vec, _resident((d, d)), _resident((d, d)), vec, vec]
        out_specs.append(row)
        out_shape.append(jax.ShapeDtypeStruct((n, d), BF16))
    else:
        tail = "q"
        g3, w_q = next_q
        per_seq = seq // tm
        args += [g3.reshape(1, d), w_q.T.astype(BF16)]
        specs += [vec, _resident((d, d))]
        out_specs.append(pl.BlockSpec((None, d, tm), lambda i: (i // per_seq, 0, i % per_seq)))
        out_shape.append(jax.ShapeDtypeStruct((n // seq, d, seq), BF16))
    out = pl.pallas_call(
        functools.partial(_ffn_ple_kernel, chunk=1024, with_attn=attn is not None, tail=tail,
                          q_scale=(d // N_HEADS) ** -0.5 * LOG2E),
        grid=(n // tm,),
        in_specs=specs,
        out_specs=out_specs,
        out_shape=out_shape,
        compiler_params=_params("parallel"),
        name="ffn_ple",
    )(*args)
    return out[0] if tail == "final" else out


def _gate_kv_kernel(h_ref, g_ref, wf_ref, bf_ref, eq_ref, onesq_ref, wk_ref, wvt_ref, ek_ref,
                    onesk_ref, qa_ref, k_ref, vt_ref, carry_ref):
    @pl.when(pl.program_id(1) == 0)
    def _():
        carry_ref[...] = jnp.zeros_like(carry_ref)

    tm = h_ref.shape[0]
    xn = _rms(h_ref[...], g_ref[...]).astype(BF16)
    vt_ref[...] = lax.dot_general(wvt_ref[...], xn, _NT,
                                  preferred_element_type=F32).astype(vt_ref.dtype)
    f = jnp.dot(xn, wf_ref[...], preferred_element_type=F32) + bf_ref[...]
    logf = jnp.minimum(f, 0.0) - jnp.log1p(jnp.exp(-jnp.abs(f)))
    rows = lax.broadcasted_iota(jnp.int32, (tm, tm), 0)
    cols = lax.broadcasted_iota(jnp.int32, (tm, tm), 1)
    tri = jnp.where(rows >= cols, 1.0, 0.0).astype(BF16)
    c = sum(jnp.dot(tri, piece, preferred_element_type=F32) for piece in _split3(logf))
    c = c + carry_ref[...]
    carry_ref[...] = c[tm - 1:tm, :]
    qa = sum(lax.dot_general(eq_ref[r], piece, _NT, preferred_element_type=F32)
             for r, piece in enumerate(_split3(c * LOG2E)))
    qa_ref[...] = (qa + onesq_ref[...]).astype(qa_ref.dtype)
    pieces = jnp.concatenate(_split3(-(c * LOG2E)), axis=1)
    k = jnp.dot(xn, wk_ref[...], preferred_element_type=F32)
    k = k + jnp.dot(pieces, ek_ref[...], preferred_element_type=F32) + onesk_ref[...]
    k_ref[...] = k.astype(k_ref.dtype)


def _gate_kv(h, g, w_k, w_v, w_f, b_f):
    b, s, d = h.shape
    nh = N_HEADS
    dh = d // nh
    heads = jnp.arange(nh)
    wf = jnp.zeros((d, LANES), F32).at[:, :nh].set(w_f).astype(BF16)
    bf = jnp.zeros((1, LANES), F32).at[0, :nh].set(b_f)
    eq = jnp.zeros((3, nh * BF16_ROWS, LANES), F32)
    onesq = jnp.zeros((nh * BF16_ROWS, 1), F32)
    ek = jnp.zeros((3, LANES, nh * LANES), F32)
    onesk = jnp.zeros((1, nh * LANES), F32)
    for r in range(3):
        eq = eq.at[r, heads * BF16_ROWS + r, heads].set(1.0)
        onesq = onesq.at[heads * BF16_ROWS + 3 + r, 0].set(1.0)
        onesk = onesk.at[0, heads * LANES + dh + r].set(1.0)
        ek = ek.at[r, heads, heads * LANES + dh + 3 + r].set(1.0)
    wk = w_k.reshape(d, nh, dh)
    wk = jnp.concatenate([wk, jnp.zeros((d, nh, LANES - dh), wk.dtype)], axis=2)
    return pl.pallas_call(
        _gate_kv_kernel,
        grid=(b, s // TM),
        in_specs=[pl.BlockSpec((None, TM, d), lambda bi, i: (bi, i, 0)), _resident((1, d)),
                  _resident((d, LANES)), _resident((1, LANES)),
                  _resident((3, nh * BF16_ROWS, LANES)), _resident((nh * BF16_ROWS, 1)),
                  _resident((d, nh * LANES)), _resident((d, d)),
                  _resident((3 * LANES, nh * LANES)), _resident((1, nh * LANES))],
        out_specs=[pl.BlockSpec((None, nh * BF16_ROWS, TM), lambda bi, i: (bi, 0, i)),
                   pl.BlockSpec((None, TM, nh * LANES), lambda bi, i: (bi, i, 0)),
                   pl.BlockSpec((None, d, TM), lambda bi, i: (bi, 0, i))],
        out_shape=[jax.ShapeDtypeStruct((b, nh * BF16_ROWS, s), BF16),
                   jax.ShapeDtypeStruct((b, s, nh * LANES), BF16),
                   jax.ShapeDtypeStruct((b, d, s), BF16)],
        scratch_shapes=[pltpu.VMEM((1, LANES), F32)],
        compiler_params=_params("parallel", "arbitrary"),
        name="gate_kv_proj",
    )(h, g.reshape(1, d), wf, bf, eq.astype(BF16), onesq, wk.reshape(d, nh * LANES).astype(BF16),
      w_v.T.astype(BF16), ek.reshape(3 * LANES, nh * LANES).astype(BF16), onesk)


def _attn_kernel(qt_ref, qa_ref, k_ref, vt_ref, o_ref, s_ref, acc_ref, *, dh, nh):
    tq = qt_ref.shape[1]
    tk = s_ref.shape[2]
    nkb = tq // tk
    assert tq == nkb * tk and nkb % 2 == 0
    i = pl.program_id(2)
    kpos = lax.broadcasted_iota(jnp.int32, (tk, tq), 0)
    qpos = lax.broadcasted_iota(jnp.int32, (tk, tq), 1)
    ones = jnp.ones((BF16_ROWS, tk), BF16)
    pad = jnp.zeros((LANES - dh - BF16_ROWS, tq), BF16)

    def scores(hh, j, slot, diag=None):
        kv = pl.ds(pl.multiple_of(j * tk, tk), tk)
        k = k_ref[kv, hh * LANES:(hh + 1) * LANES]
        qt = jnp.concatenate([qt_ref[hh * dh:(hh + 1) * dh, :],
                              qa_ref[hh * BF16_ROWS:(hh + 1) * BF16_ROWS, :], pad], axis=0)
        s = jnp.dot(k, qt, preferred_element_type=F32)
        if diag is not None:
            s = jnp.where(kpos + diag * tk <= qpos, s, NEG_BIG)
        s_ref[slot, hh] = s
        return jnp.max(s, axis=0, keepdims=True)

    def update(hh, j, slot, bmax, carry):
        m = carry
        kv = pl.ds(pl.multiple_of(j * tk, tk), tk)
        vt = jnp.concatenate([vt_ref[hh * dh:(hh + 1) * dh, kv], ones], axis=0)
        m_new = jnp.maximum(m, bmax)
        alpha = jnp.exp2(m - m_new)
        p = jnp.exp2((s_ref[slot, hh] - m_new).astype(BF16))
        acc_ref[hh] = alpha * acc_ref[hh] + jnp.dot(vt, p, preferred_element_type=F32)
        return m_new

    def half(j, j_next, slot, carry, diag_next=None):
        stats, bmax = carry
        new_stats, new_bmax = [], []
        for hh in range(nh):
            new_bmax.append(scores(hh, j_next, 1 - slot, diag_next))
            new_stats.append(update(hh, j, slot, bmax[hh], stats[hh]))
        return tuple(new_stats), tuple(new_bmax)

    def step(t, carry):
        for d in range(nkb):
            carry = half(nkb * t + d, nkb * t + d + 1, d % 2, carry)
        return carry

    acc_ref[...] = jnp.zeros_like(acc_ref)
    init = jnp.full((1, tq), NEG_BIG, F32)
    carry = (init,) * nh, tuple(scores(hh, nkb * i, 0, 0) for hh in range(nh))
    for d in range(nkb - 1):
        carry = half(nkb * i + d, nkb * i + d + 1, d % 2, carry, d + 1)
    carry = half(nkb * i + nkb - 1, 0, 1, carry)
    lax.fori_loop(0, i, step, carry)
    outs = [acc_ref[hh, :dh] / acc_ref[hh, dh:dh + 1] for hh in range(nh)]
    o_ref[...] = jnp.concatenate(outs, axis=0).T.astype(o_ref.dtype)


def _attention(qt, qa, k_aug, vt):
    b, d, s = vt.shape
    nh = ATTN_HEADS_PER_STEP
    dh = d // N_HEADS
    return pl.pallas_call(
        functools.partial(_attn_kernel, dh=dh, nh=nh),
        grid=(b, N_HEADS // nh, s // TQ),
        in_specs=[pl.BlockSpec((None, nh * dh, TQ), lambda bi, hg, i: (bi, hg, i)),
                  pl.BlockSpec((None, nh * BF16_ROWS, TQ), lambda bi, hg, i: (bi, hg, i)),
                  pl.BlockSpec((None, s, nh * LANES), lambda bi, hg, i: (bi, 0, hg)),
                  pl.BlockSpec((None, nh * dh, s), lambda bi, hg, i: (bi, hg, 0))],
        out_specs=pl.BlockSpec((None, TQ, nh * dh), lambda bi, hg, i: (bi, i, hg)),
        out_shape=jax.ShapeDtypeStruct((b, s, d), BF16),
        scratch_shapes=[pltpu.VMEM((2, nh, TK, TQ), F32),
                        pltpu.VMEM((nh, dh + BF16_ROWS, TQ), F32)],
        compiler_params=_params("parallel", "parallel", "arbitrary"),
        name="fox_attention",
    )(qt, qa, k_aug, vt)


def kernel(x, p, mix_norm, conv_w_pw1, conv_b_pw1, conv_w_dw, conv_b_dw, conv_ln_g, conv_ln_b,
           conv_w_pw2, conv_b_pw2, kv_norm, w_kvf, b_f, attn_w_q, attn_w_o, ffn_norm, ffn_w1,
           ffn_w2, ple_norm, ple_w_gate, ple_w_proj, final_norm):
    b, s, d = x.shape
    n = b * s
    depth = mix_norm.shape[0]
    n_a = conv_w_pw1.shape[0]
    assert s % TQ == 0 and s % TM == 0 and s % TM_CONV == 0
    assert d == N_HEADS * (d // N_HEADS) and d // N_HEADS + BF16_ROWS <= LANES

    h = x.reshape(n, d)
    p = p.reshape(depth, n, -1)
    ffn_w1, ffn_w2 = ffn_w1.astype(BF16), ffn_w2.astype(BF16)
    ple_w_gate, ple_w_proj = ple_w_gate.astype(BF16), ple_w_proj.astype(BF16)
    assert 1 <= n_a < depth
    qa = k_aug = vt = None
    glu = _pw1_glu(h, mix_norm[0], conv_w_pw1[0], conv_b_pw1[0])
    qt = None
    for i in range(depth):
        attn = None
        if i < n_a:
            h = _conv_block(glu.reshape(b, s, d), h.reshape(b, s, d), conv_w_dw[i], conv_b_dw[i],
                            conv_ln_g[i], conv_ln_b[i], conv_w_pw2[i], conv_b_pw2[i]).reshape(n, d)
        else:
            j = i - n_a
            if j == 0:
                qa, k_aug, vt = _gate_kv(h.reshape(b, s, d), kv_norm, w_kvf[:, :d],
                                         w_kvf[:, d:2 * d], w_kvf[:, 2 * d:], b_f)
            attn = (_attention(qt, qa, k_aug, vt).reshape(n, d), attn_w_o[j])
        tail = {}
        if i == depth - 1:
            tail["final_g"] = final_norm
        elif i + 1 < n_a:
            tail["next_glu"] = (mix_norm[i + 1], conv_w_pw1[i + 1], conv_b_pw1[i + 1])
        else:
            tail["next_q"] = (mix_norm[i + 1], attn_w_q[i + 1 - n_a])
        out = _ffn_ple(h, i, p, ffn_norm[i], ffn_w1, ffn_w2, ple_norm[i], ple_w_gate, ple_w_proj,
                       attn=attn, seq=s, **tail)
        if i == depth - 1:
            h = out
        elif i + 1 < n_a:
            h, glu = out
        else:
            h, qt = out
    return h.reshape(b, s, d)
```

```python
import functools
import math

import jax
import jax.numpy as jnp
from jax import lax
from jax.experimental import pallas as pl
from jax.experimental.pallas import tpu as pltpu

F32 = jnp.float32
BF16 = jnp.bfloat16

EPS = 1e-6
NEG_BIG = -1e30
N_HEADS = 16
CONV_WIDTH = 31
LOG2E = math.log2(math.e)

LANES = 128
BF16_ROWS = 16
HALO = 32
TM = 512
TM_FFN = 512
TM_CONV = 512
TQ = 512
TK = 256
QCHUNK = 256
ATTN_HEADS_PER_STEP = 8
VMEM_LIMIT = 56 * 1024 * 1024

_NT = (((1,), (1,)), ((), ()))


def _params(*sem):
    return pltpu.CompilerParams(dimension_semantics=sem, vmem_limit_bytes=VMEM_LIMIT)


def _resident(shape):
    nd = len(shape)
    return pl.BlockSpec(shape, lambda *_: (0,) * nd, pipeline_mode=pl.Buffered(1))


def _rms(x, g):
    return x * lax.rsqrt(jnp.mean(x * x, axis=-1, keepdims=True) + EPS) * g


def _split3(x):
    hi = x.astype(BF16)
    r = x - hi.astype(F32)
    mid = r.astype(BF16)
    lo = (r - mid.astype(F32)).astype(BF16)
    return hi, mid, lo


def _glu(xn, wa_ref, wg_ref, ba_ref, bg_ref):
    a = jnp.dot(xn, wa_ref[...], preferred_element_type=F32) + ba_ref[...]
    gt = jnp.dot(xn, wg_ref[...], preferred_element_type=F32) + bg_ref[...]
    return a * jax.nn.sigmoid(gt)


def _pw1_glu_kernel(h_ref, g_ref, wa_ref, wg_ref, ba_ref, bg_ref, o_ref):
    xn = _rms(h_ref[...], g_ref[...]).astype(BF16)
    o_ref[...] = _glu(xn, wa_ref, wg_ref, ba_ref, bg_ref).astype(o_ref.dtype)


def _pw1_glu(h, g, w, b):
    n, d = h.shape
    wa, wg = w[:, :d].astype(BF16), w[:, d:].astype(BF16)
    ba, bg = b[:d].reshape(1, d), b[d:].reshape(1, d)
    row = pl.BlockSpec((TM, d), lambda i: (i, 0))
    return pl.pallas_call(
        _pw1_glu_kernel,
        grid=(n // TM,),
        in_specs=[row, _resident((1, d)), _resident((d, d)), _resident((d, d)),
                  _resident((1, d)), _resident((1, d))],
        out_specs=row,
        out_shape=jax.ShapeDtypeStruct((n, d), BF16),
        compiler_params=_params("parallel"),
        name="pw1_glu",
    )(h, g.reshape(1, d), wa, wg, ba, bg)


def _conv_kernel(x_ref, halo_ref, h_ref, wdw_ref, bdw_ref, lg_ref, lb_ref, w2_ref, b2_ref,
                 o_ref, xs_ref, y_ref):
    tm, d = x_ref.shape
    first = pl.program_id(1) == 0
    xs_ref[0:HALO, :] = jnp.where(first, 0.0, halo_ref[...].astype(F32))
    xs_ref[HALO:, :] = x_ref[...].astype(F32)

    groups = [[] for _ in range(8)]
    for k in range(CONV_WIDTH):
        a, b = divmod(HALO - (CONV_WIDTH - 1) + k, 8)
        groups[b].append((a, k))
    row = lax.broadcasted_iota(jnp.int32, (8, LANES), 0)

    def strip(c, carry):
        lanes = pl.ds(pl.multiple_of(c * LANES, LANES), LANES)
        w = [jnp.broadcast_to(wdw_ref[k:k + 1, lanes], (8, LANES)) for k in range(CONV_WIDTH)]
        bias = jnp.broadcast_to(bdw_ref[:, lanes], (8, LANES))

        def u(b, chunk):
            acc = None
            for a, k in groups[b]:
                term = w[k] * xs_ref[(chunk + a) * 8:(chunk + a + 1) * 8, lanes]
                acc = term if acc is None else acc + term
            return acc

        def rotated(chunk):
            return [pltpu.roll(u(b, chunk), 8 - b, 0) for b in range(1, 8)]

        cur = rotated(0)
        for chunk in range(tm // 8):
            nxt = rotated(chunk + 1)
            acc = bias + u(0, chunk)
            for b in range(1, 8):
                acc = acc + jnp.where(row < 8 - b, cur[b - 1], nxt[b - 1])
            y_ref[chunk * 8:(chunk + 1) * 8, lanes] = acc
            cur = nxt
        return carry

    lax.fori_loop(0, d // LANES, strip, 0)

    y = y_ref[...]
    mu = jnp.mean(y, axis=-1, keepdims=True)
    yc = y - mu
    var = jnp.mean(yc * yc, axis=-1, keepdims=True)
    z = yc * lax.rsqrt(var + EPS) * lg_ref[...] + lb_ref[...]
    z = z * jax.nn.sigmoid(z)
    out = jnp.dot(z.astype(BF16), w2_ref[...], preferred_element_type=F32) + b2_ref[...]
    o_ref[...] = h_ref[...] + out


def _conv_block(glu, h, w_dw, b_dw, ln_g, ln_b, w_pw2, b_pw2):
    b, s, d = glu.shape
    tm = TM_CONV
    per = tm // HALO
    wdw = jnp.zeros((HALO, d), F32).at[:CONV_WIDTH].set(w_dw)
    row = pl.BlockSpec((None, tm, d), lambda bi, i: (bi, i, 0))
    halo = pl.BlockSpec((None, HALO, d), lambda bi, i: (bi, jnp.maximum(i * per - 1, 0), 0))
    vec = _resident((1, d))
    return pl.pallas_call(
        _conv_kernel,
        grid=(b, s // tm),
        in_specs=[row, halo, row, _resident((HALO, d)), vec, vec, vec, _resident((d, d)), vec],
        out_specs=row,
        out_shape=jax.ShapeDtypeStruct((b, s, d), F32),
        scratch_shapes=[pltpu.VMEM((tm + HALO, d), F32), pltpu.VMEM((tm, d), F32)],
        compiler_params=_params("parallel", "parallel"),
        name="dwconv_ln_pw2",
    )(glu, glu, h, wdw, b_dw.reshape(1, d), ln_g.reshape(1, d), ln_b.reshape(1, d),
      w_pw2.astype(BF16), b_pw2.reshape(1, d))


def _ffn_ple_kernel(*refs, chunk, with_attn, tail, q_scale):
    refs = list(refs)
    h_ref = refs.pop(0)
    h = h_ref[...]
    if with_attn:
        o_ref, wo_ref = refs.pop(0), refs.pop(0)
        h = h + jnp.dot(o_ref[...], wo_ref[...], preferred_element_type=F32)
    p_ref, g1_ref, w1_ref, w2_ref, g2_ref, wg_ref, wp_ref = refs[:7]
    refs = refs[7:]
    xn = _rms(h, g1_ref[...]).astype(BF16)
    for c in range(w1_ref.shape[1] // chunk):
        a = jnp.dot(xn, w1_ref[:, c * chunk:(c + 1) * chunk], preferred_element_type=F32)
        a = jnp.square(jnp.maximum(a, 0.0)).astype(BF16)
        h = h + jnp.dot(a, w2_ref[c * chunk:(c + 1) * chunk, :], preferred_element_type=F32)
    xn = _rms(h, g2_ref[...]).astype(BF16)
    gate = jax.nn.sigmoid(jnp.dot(xn, wg_ref[...], preferred_element_type=F32))
    proj = jnp.dot(p_ref[...].astype(BF16), wp_ref[...], preferred_element_type=F32)
    h = h + gate * proj
    if tail == "final":
        gf_ref, out_ref = refs
        out_ref[...] = _rms(h, gf_ref[...])
    elif tail == "glu":
        g3_ref, wa_ref, wgl_ref, ba_ref, bg_ref, out_ref, glu_ref = refs
        out_ref[...] = h
        xn = _rms(h, g3_ref[...]).astype(BF16)
        glu_ref[...] = _glu(xn, wa_ref, wgl_ref, ba_ref, bg_ref).astype(glu_ref.dtype)
    else:
        g3_ref, wqt_ref, out_ref, qt_ref = refs
        out_ref[...] = h
        xn = _rms(h, g3_ref[...]).astype(BF16)
        qt = lax.dot_general(wqt_ref[...], xn, _NT, preferred_element_type=F32) * q_scale
        qt_ref[...] = qt.astype(qt_ref.dtype)


def _ffn_ple(h, layer, p, g1, w1, w2, g2, w_gate, w_proj, attn=None, final_g=None, next_glu=None,
             next_q=None, seq=None):
    n, d = h.shape
    dff = w1.shape[2]
    dp = p.shape[2]
    tm = TM_FFN
    row = pl.BlockSpec((tm, d), lambda i: (i, 0))
    vec = _resident((1, d))

    def of_layer(shape):
        return pl.BlockSpec((None,) + shape, lambda i: (layer, 0, 0), pipeline_mode=pl.Buffered(1))

    args, specs = [h], [row]
    if attn is not None:
        o, w_o = attn
        args += [o, w_o.astype(BF16)]
        specs += [row, _resident((d, d))]
    args += [p, g1.reshape(1, d), w1, w2, g2.reshape(1, d), w_gate, w_proj]
    specs += [pl.BlockSpec((None, tm, dp), lambda i: (layer, i, 0)), vec, of_layer((d, dff)),
              of_layer((dff, d)), vec, of_layer((d, d)), of_layer((dp, d))]
    out_specs, out_shape = [row], [jax.ShapeDtypeStruct((n, d), F32)]
    if final_g is not None:
        tail = "final"
        args.append(final_g.reshape(1, d))
        specs.append(vec)
    elif next_glu is not None:
        tail = "glu"
        g3, w, bias = next_glu
        args += [g3.reshape(1, d), w[:, :d].astype(BF16), w[:, d:].astype(BF16),
                 bias[:d].reshape(1, d), bias[d:].reshape(1, d)]
        specs += [vec, _resident((d, d)), _resident((d, d)), vec, vec]
        out_specs.append(row)
        out_shape.append(jax.ShapeDtypeStruct((n, d), BF16))
    else:
        tail = "q"
        g3, w_q = next_q
        per_seq = seq // tm
        args += [g3.reshape(1, d), w_q.T.astype(BF16)]
        specs += [vec, _resident((d, d))]
        out_specs.append(pl.BlockSpec((None, d, tm), lambda i: (i // per_seq, 0, i % per_seq)))
        out_shape.append(jax.ShapeDtypeStruct((n // seq, d, seq), BF16))
    out = pl.pallas_call(
        functools.partial(_ffn_ple_kernel, chunk=1024, with_attn=attn is not None, tail=tail,
                          q_scale=(d // N_HEADS) ** -0.5 * LOG2E),
        grid=(n // tm,),
        in_specs=specs,
        out_specs=out_specs,
        out_shape=out_shape,
        compiler_params=_params("parallel"),
        name="ffn_ple",
    )(*args)
    return out[0] if tail == "final" else out


def _gate_kv_kernel(h_ref, g_ref, wf_ref, bf_ref, eq_ref, onesq_ref, wk_ref, wvt_ref, ek_ref,
                    onesk_ref, qa_ref, k_ref, vt_ref, carry_ref):
    @pl.when(pl.program_id(1) == 0)
    def _():
        carry_ref[...] = jnp.zeros_like(carry_ref)

    tm = h_ref.shape[0]
    xn = _rms(h_ref[...], g_ref[...]).astype(BF16)
    vt_ref[...] = lax.dot_general(wvt_ref[...], xn, _NT,
                                  preferred_element_type=F32).astype(vt_ref.dtype)
    f = jnp.dot(xn, wf_ref[...], preferred_element_type=F32) + bf_ref[...]
    logf = jnp.minimum(f, 0.0) - jnp.log1p(jnp.exp(-jnp.abs(f)))
    rows = lax.broadcasted_iota(jnp.int32, (tm, tm), 0)
    cols = lax.broadcasted_iota(jnp.int32, (tm, tm), 1)
    tri = jnp.where(rows >= cols, 1.0, 0.0).astype(BF16)
    c = sum(jnp.dot(tri, piece, preferred_element_type=F32) for piece in _split3(logf))
    c = c + carry_ref[...]
    carry_ref[...] = c[tm - 1:tm, :]
    qa = sum(lax.dot_general(eq_ref[r], piece, _NT, preferred_element_type=F32)
             for r, piece in enumerate(_split3(c * LOG2E)))
    qa_ref[...] = (qa + onesq_ref[...]).astype(qa_ref.dtype)
    pieces = jnp.concatenate(_split3(-(c * LOG2E)), axis=1)
    k = jnp.dot(xn, wk_ref[...], preferred_element_type=F32)
    k = k + jnp.dot(pieces, ek_ref[...], preferred_element_type=F32) + onesk_ref[...]
    k_ref[...] = k.astype(k_ref.dtype)


def _gate_kv(h, g, w_k, w_v, w_f, b_f):
    b, s, d = h.shape
    nh = N_HEADS
    dh = d // nh
    heads = jnp.arange(nh)
    wf = jnp.zeros((d, LANES), F32).at[:, :nh].set(w_f).astype(BF16)
    bf = jnp.zeros((1, LANES), F32).at[0, :nh].set(b_f)
    eq = jnp.zeros((3, nh * BF16_ROWS, LANES), F32)
    onesq = jnp.zeros((nh * BF16_ROWS, 1), F32)
    ek = jnp.zeros((3, LANES, nh * LANES), F32)
    onesk = jnp.zeros((1, nh * LANES), F32)
    for r in range(3):
        eq = eq.at[r, heads * BF16_ROWS + r, heads].set(1.0)
        onesq = onesq.at[heads * BF16_ROWS + 3 + r, 0].set(1.0)
        onesk = onesk.at[0, heads * LANES + dh + r].set(1.0)
        ek = ek.at[r, heads, heads * LANES + dh + 3 + r].set(1.0)
    wk = w_k.reshape(d, nh, dh)
    wk = jnp.concatenate([wk, jnp.zeros((d, nh, LANES - dh), wk.dtype)], axis=2)
    return pl.pallas_call(
        _gate_kv_kernel,
        grid=(b, s // TM),
        in_specs=[pl.BlockSpec((None, TM, d), lambda bi, i: (bi, i, 0)), _resident((1, d)),
                  _resident((d, LANES)), _resident((1, LANES)),
                  _resident((3, nh * BF16_ROWS, LANES)), _resident((nh * BF16_ROWS, 1)),
                  _resident((d, nh * LANES)), _resident((d, d)),
                  _resident((3 * LANES, nh * LANES)), _resident((1, nh * LANES))],
        out_specs=[pl.BlockSpec((None, nh * BF16_ROWS, TM), lambda bi, i: (bi, 0, i)),
                   pl.BlockSpec((None, TM, nh * LANES), lambda bi, i: (bi, i, 0)),
                   pl.BlockSpec((None, d, TM), lambda bi, i: (bi, 0, i))],
        out_shape=[jax.ShapeDtypeStruct((b, nh * BF16_ROWS, s), BF16),
                   jax.ShapeDtypeStruct((b, s, nh * LANES), BF16),
                   jax.ShapeDtypeStruct((b, d, s), BF16)],
        scratch_shapes=[pltpu.VMEM((1, LANES), F32)],
        compiler_params=_params("parallel", "arbitrary"),
        name="gate_kv_proj",
    )(h, g.reshape(1, d), wf, bf, eq.astype(BF16), onesq, wk.reshape(d, nh * LANES).astype(BF16),
      w_v.T.astype(BF16), ek.reshape(3 * LANES, nh * LANES).astype(BF16), onesk)


def _attn_kernel(qt_ref, qa_ref, k_ref, vt_ref, o_ref, s_ref, *, dh, nh):
    tq = qt_ref.shape[1]
    tk = s_ref.shape[2]
    nkb = tq // tk
    assert tq == nkb * tk and nkb % 2 == 0
    i = pl.program_id(2)
    kpos = lax.broadcasted_iota(jnp.int32, (tk, tq), 0)
    qpos = lax.broadcasted_iota(jnp.int32, (tk, tq), 1)
    ones = jnp.ones((BF16_ROWS, tk), BF16)
    pad = jnp.zeros((LANES - dh - BF16_ROWS, tq), BF16)

    def scores(hh, j, slot, diag=None):
        kv = pl.ds(pl.multiple_of(j * tk, tk), tk)
        k = k_ref[kv, hh * LANES:(hh + 1) * LANES]
        qt = jnp.concatenate([qt_ref[hh * dh:(hh + 1) * dh, :],
                              qa_ref[hh * BF16_ROWS:(hh + 1) * BF16_ROWS, :], pad], axis=0)
        s = jnp.dot(k, qt, preferred_element_type=F32)
        if diag is not None:
            s = jnp.where(kpos + diag * tk <= qpos, s, NEG_BIG)
        s_ref[slot, hh] = s
        return jnp.max(s, axis=0, keepdims=True)

    def update(hh, j, slot, bmax, carry):
        m, acc = carry
        kv = pl.ds(pl.multiple_of(j * tk, tk), tk)
        vt = jnp.concatenate([vt_ref[hh * dh:(hh + 1) * dh, kv], ones], axis=0)
        m_new = jnp.maximum(m, bmax)
        alpha = jnp.exp2(m - m_new)
        new_acc = []
        for c, acc_c in enumerate(acc):
            cols = slice(c * QCHUNK, (c + 1) * QCHUNK)
            p = jnp.exp2((s_ref[slot, hh, :, cols] - m_new[:, cols]).astype(BF16))
            new_acc.append(alpha[:, cols] * acc_c + jnp.dot(vt, p, preferred_element_type=F32))
        return m_new, tuple(new_acc)

    def half(j, j_next, slot, carry, diag_next=None):
        stats, bmax = carry
        new_stats, new_bmax = [], []
        for hh in range(nh):
            new_bmax.append(scores(hh, j_next, 1 - slot, diag_next))
            new_stats.append(update(hh, j, slot, bmax[hh], stats[hh]))
        return tuple(new_stats), tuple(new_bmax)

    def step(t, carry):
        for d in range(nkb):
            carry = half(nkb * t + d, nkb * t + d + 1, d % 2, carry)
        return carry

    init = (jnp.full((1, tq), NEG_BIG, F32),
            (jnp.zeros((dh + BF16_ROWS, QCHUNK), F32),) * (tq // QCHUNK))
    carry = (init,) * nh, tuple(scores(hh, nkb * i, 0, 0) for hh in range(nh))
    for d in range(nkb - 1):
        carry = half(nkb * i + d, nkb * i + d + 1, d % 2, carry, d + 1)
    carry = half(nkb * i + nkb - 1, 0, 1, carry)
    stats, _ = lax.fori_loop(0, i, step, carry)
    outs = [jnp.concatenate([a[:dh] / a[dh:dh + 1] for a in acc], axis=1) for (_, acc) in stats]
    o_ref[...] = jnp.concatenate(outs, axis=0).T.astype(o_ref.dtype)


def _attention(qt, qa, k_aug, vt):
    b, d, s = vt.shape
    nh = ATTN_HEADS_PER_STEP
    dh = d // N_HEADS
    return pl.pallas_call(
        functools.partial(_attn_kernel, dh=dh, nh=nh),
        grid=(b, N_HEADS // nh, s // TQ),
        in_specs=[pl.BlockSpec((None, nh * dh, TQ), lambda bi, hg, i: (bi, hg, i)),
                  pl.BlockSpec((None, nh * BF16_ROWS, TQ), lambda bi, hg, i: (bi, hg, i)),
                  pl.BlockSpec((None, s, nh * LANES), lambda bi, hg, i: (bi, 0, hg),
                               pipeline_mode=pl.Buffered(1)),
                  pl.BlockSpec((None, nh * dh, s), lambda bi, hg, i: (bi, hg, 0))],
        out_specs=pl.BlockSpec((None, TQ, nh * dh), lambda bi, hg, i: (bi, i, hg)),
        out_shape=jax.ShapeDtypeStruct((b, s, d), BF16),
        scratch_shapes=[pltpu.VMEM((2, nh, TK, TQ), F32)],
        compiler_params=_params("parallel", "parallel", "arbitrary"),
        name="fox_attention",
    )(qt, qa, k_aug, vt)


def kernel(x, p, mix_norm, conv_w_pw1, conv_b_pw1, conv_w_dw, conv_b_dw, conv_ln_g, conv_ln_b,
           conv_w_pw2, conv_b_pw2, kv_norm, w_kvf, b_f, attn_w_q, attn_w_o, ffn_norm, ffn_w1,
           ffn_w2, ple_norm, ple_w_gate, ple_w_proj, final_norm):
    b, s, d = x.shape
    n = b * s
    depth = mix_norm.shape[0]
    n_a = conv_w_pw1.shape[0]
    assert s % TQ == 0 and s % TM == 0 and s % TM_CONV == 0
    assert d == N_HEADS * (d // N_HEADS) and d // N_HEADS + BF16_ROWS <= LANES

    h = x.reshape(n, d)
    p = p.reshape(depth, n, -1)
    ffn_w1, ffn_w2 = ffn_w1.astype(BF16), ffn_w2.astype(BF16)
    ple_w_gate, ple_w_proj = ple_w_gate.astype(BF16), ple_w_proj.astype(BF16)
    assert 1 <= n_a < depth
    qa = k_aug = vt = None
    glu = _pw1_glu(h, mix_norm[0], conv_w_pw1[0], conv_b_pw1[0])
    qt = None
    for i in range(depth):
        attn = None
        if i < n_a:
            h = _conv_block(glu.reshape(b, s, d), h.reshape(b, s, d), conv_w_dw[i], conv_b_dw[i],
                            conv_ln_g[i], conv_ln_b[i], conv_w_pw2[i], conv_b_pw2[i]).reshape(n, d)
        else:
            j = i - n_a
            if j == 0:
                qa, k_aug, vt = _gate_kv(h.reshape(b, s, d), kv_norm, w_kvf[:, :d],
                                         w_kvf[:, d:2 * d], w_kvf[:, 2 * d:], b_f)
            attn = (_attention(qt, qa, k_aug, vt).reshape(n, d), attn_w_o[j])
        tail = {}
        if i == depth - 1:
            tail["final_g"] = final_norm
        elif i + 1 < n_a:
            tail["next_glu"] = (mix_norm[i + 1], conv_w_pw1[i + 1], conv_b_pw1[i + 1])
        else:
            tail["next_q"] = (mix_norm[i + 1], attn_w_q[i + 1 - n_a])
        out = _ffn_ple(h, i, p, ffn_norm[i], ffn_w1, ffn_w2, ple_norm[i], ple_w_gate, ple_w_proj,
                       attn=attn, seq=s, **tail)
        if i == depth - 1:
            h = out
        elif i + 1 < n_a:
            h, glu = out
        else:
            h, qt = out
    return h.reshape(b, s, d)
```

```python
import functools
import math

import jax
import jax.numpy as jnp
from jax import lax
from jax.experimental import pallas as pl
from jax.experimental.pallas import tpu as pltpu

F32 = jnp.float32
BF16 = jnp.bfloat16

EPS = 1e-6
NEG_BIG = -1e30
N_HEADS = 16
CONV_WIDTH = 31
LOG2E = math.log2(math.e)

LANES = 128
BF16_ROWS = 16
HALO = 32
TM = 512
TM_FFN = 512
TM_CONV = 512
TQ = 512
TK = 256
QCHUNK = 256
ATTN_HEADS_PER_STEP = 8
VMEM_LIMIT = 56 * 1024 * 1024

_NT = (((1,), (1,)), ((), ()))


def _params(*sem):
    return pltpu.CompilerParams(dimension_semantics=sem, vmem_limit_bytes=VMEM_LIMIT)


def _resident(shape):
    nd = len(shape)
    return pl.BlockSpec(shape, lambda *_: (0,) * nd, pipeline_mode=pl.Buffered(1))


def _rms(x, g):
    return x * lax.rsqrt(jnp.mean(x * x, axis=-1, keepdims=True) + EPS) * g


def _split3(x):
    hi = x.astype(BF16)
    r = x - hi.astype(F32)
    mid = r.astype(BF16)
    lo = (r - mid.astype(F32)).astype(BF16)
    return hi, mid, lo


def _glu(xn, wa_ref, wg_ref, ba_ref, bg_ref):
    a = jnp.dot(xn, wa_ref[...], preferred_element_type=F32) + ba_ref[...]
    gt = jnp.dot(xn, wg_ref[...], preferred_element_type=F32) + bg_ref[...]
    return a * jax.nn.sigmoid(gt)


def _pw1_glu_kernel(h_ref, g_ref, wa_ref, wg_ref, ba_ref, bg_ref, o_ref):
    xn = _rms(h_ref[...], g_ref[...]).astype(BF16)
    o_ref[...] = _glu(xn, wa_ref, wg_ref, ba_ref, bg_ref).astype(o_ref.dtype)


def _pw1_glu(h, g, w, b):
    n, d = h.shape
    wa, wg = w[:, :d].astype(BF16), w[:, d:].astype(BF16)
    ba, bg = b[:d].reshape(1, d), b[d:].reshape(1, d)
    row = pl.BlockSpec((TM, d), lambda i: (i, 0))
    return pl.pallas_call(
        _pw1_glu_kernel,
        grid=(n // TM,),
        in_specs=[row, _resident((1, d)), _resident((d, d)), _resident((d, d)),
                  _resident((1, d)), _resident((1, d))],
        out_specs=row,
        out_shape=jax.ShapeDtypeStruct((n, d), BF16),
        compiler_params=_params("parallel"),
        name="pw1_glu",
    )(h, g.reshape(1, d), wa, wg, ba, bg)


def _conv_kernel(x_ref, halo_ref, h_ref, wdw_ref, bdw_ref, lg_ref, lb_ref, w2_ref, b2_ref,
                 o_ref, xs_ref, y_ref):
    tm, d = x_ref.shape
    first = pl.program_id(1) == 0
    xs_ref[0:HALO, :] = jnp.where(first, 0.0, halo_ref[...].astype(F32))
    xs_ref[HALO:, :] = x_ref[...].astype(F32)

    groups = [[] for _ in range(8)]
    for k in range(CONV_WIDTH):
        a, b = divmod(HALO - (CONV_WIDTH - 1) + k, 8)
        groups[b].append((a, k))
    row = lax.broadcasted_iota(jnp.int32, (8, LANES), 0)

    def strip(c, carry):
        lanes = pl.ds(pl.multiple_of(c * LANES, LANES), LANES)
        w = [jnp.broadcast_to(wdw_ref[k:k + 1, lanes], (8, LANES)) for k in range(CONV_WIDTH)]
        bias = jnp.broadcast_to(bdw_ref[:, lanes], (8, LANES))

        def u(b, chunk):
            acc = None
            for a, k in groups[b]:
                term = w[k] * xs_ref[(chunk + a) * 8:(chunk + a + 1) * 8, lanes]
                acc = term if acc is None else acc + term
            return acc

        def rotated(chunk):
            return [pltpu.roll(u(b, chunk), 8 - b, 0) for b in range(1, 8)]

        cur = rotated(0)
        for chunk in range(tm // 8):
            nxt = rotated(chunk + 1)
            acc = bias + u(0, chunk)
            for b in range(1, 8):
                acc = acc + jnp.where(row < 8 - b, cur[b - 1], nxt[b - 1])
            y_ref[chunk * 8:(chunk + 1) * 8, lanes] = acc
            cur = nxt
        return carry

    lax.fori_loop(0, d // LANES, strip, 0)

    y = y_ref[...]
    mu = jnp.mean(y, axis=-1, keepdims=True)
    yc = y - mu
    var = jnp.mean(yc * yc, axis=-1, keepdims=True)
    z = yc * lax.rsqrt(var + EPS) * lg_ref[...] + lb_ref[...]
    z = z * jax.nn.sigmoid(z)
    out = jnp.dot(z.astype(BF16), w2_ref[...], preferred_element_type=F32) + b2_ref[...]
    o_ref[...] = h_ref[...] + out


def _conv_block(glu, h, w_dw, b_dw, ln_g, ln_b, w_pw2, b_pw2):
    b, s, d = glu.shape
    tm = TM_CONV
    per = tm // HALO
    wdw = jnp.zeros((HALO, d), F32).at[:CONV_WIDTH].set(w_dw)
    row = pl.BlockSpec((None, tm, d), lambda bi, i: (bi, i, 0))
    halo = pl.BlockSpec((None, HALO, d), lambda bi, i: (bi, jnp.maximum(i * per - 1, 0), 0))
    vec = _resident((1, d))
    return pl.pallas_call(
        _conv_kernel,
        grid=(b, s // tm),
        in_specs=[row, halo, row, _resident((HALO, d)), vec, vec, vec, _resident((d, d)), vec],
        out_specs=row,
        out_shape=jax.ShapeDtypeStruct((b, s, d), F32),
        scratch_shapes=[pltpu.VMEM((tm + HALO, d), F32), pltpu.VMEM((tm, d), F32)],
        compiler_params=_params("parallel", "parallel"),
        name="dwconv_ln_pw2",
    )(glu, glu, h, wdw, b_dw.reshape(1, d), ln_g.reshape(1, d), ln_b.reshape(1, d),
      w_pw2.astype(BF16), b_pw2.reshape(1, d))


def _ffn_ple_kernel(*refs, chunk, with_attn, tail, q_scale, tiles_per_seq):
    refs = list(refs)
    h_ref = refs.pop(0)
    h = h_ref[...]
    if with_attn:
        o_ref, wo_ref = refs.pop(0), refs.pop(0)
        h = h + jnp.dot(o_ref[...], wo_ref[...], preferred_element_type=F32)
    p_ref, g1_ref, w1_ref, w2_ref, g2_ref, wg_ref, wp_ref = refs[:7]
    refs = refs[7:]
    xn = _rms(h, g1_ref[...]).astype(BF16)
    for c in range(w1_ref.shape[1] // chunk):
        a = jnp.dot(xn, w1_ref[:, c * chunk:(c + 1) * chunk], preferred_element_type=F32)
        a = jnp.square(jnp.maximum(a, 0.0)).astype(BF16)
        h = h + jnp.dot(a, w2_ref[c * chunk:(c + 1) * chunk, :], preferred_element_type=F32)
    xn = _rms(h, g2_ref[...]).astype(BF16)
    gate = jax.nn.sigmoid(jnp.dot(xn, wg_ref[...], preferred_element_type=F32))
    proj = jnp.dot(p_ref[...].astype(BF16), wp_ref[...], preferred_element_type=F32)
    h = h + gate * proj
    if tail == "final":
        gf_ref, out_ref = refs
        out_ref[...] = _rms(h, gf_ref[...])
    elif tail == "glu":
        g3_ref, wa_ref, wgl_ref, ba_ref, bg_ref, out_ref, glu_ref = refs
        out_ref[...] = h
        xn = _rms(h, g3_ref[...]).astype(BF16)
        glu_ref[...] = _glu(xn, wa_ref, wgl_ref, ba_ref, bg_ref).astype(glu_ref.dtype)
    else:
        g3_ref, wqt_ref = refs[:2]
        out_ref, qt_ref = (refs[11:13] if tail == "q_kv" else refs[2:4])
        out_ref[...] = h
        xn = _rms(h, g3_ref[...]).astype(BF16)
        qt = lax.dot_general(wqt_ref[...], xn, _NT, preferred_element_type=F32) * q_scale
        qt_ref[...] = qt.astype(qt_ref.dtype)
        if tail == "q_kv":
            first = pl.program_id(0) % tiles_per_seq == 0
            _gate_kv_tile(h, first, *refs[2:11], *refs[13:])


def _ffn_ple(h, layer, p, g1, w1, w2, g2, w_gate, w_proj, attn=None, final_g=None, next_glu=None,
             next_q=None, next_kv=None, seq=None):
    n, d = h.shape
    dff = w1.shape[2]
    dp = p.shape[2]
    tm = TM_FFN
    row = pl.BlockSpec((tm, d), lambda i: (i, 0))
    vec = _resident((1, d))

    def of_layer(shape):
        return pl.BlockSpec((None,) + shape, lambda i: (layer, 0, 0), pipeline_mode=pl.Buffered(1))

    args, specs = [h], [row]
    if attn is not None:
        o, w_o = attn
        args += [o, w_o.astype(BF16)]
        specs += [row, _resident((d, d))]
    args += [p, g1.reshape(1, d), w1, w2, g2.reshape(1, d), w_gate, w_proj]
    specs += [pl.BlockSpec((None, tm, dp), lambda i: (layer, i, 0)), vec, of_layer((d, dff)),
              of_layer((dff, d)), vec, of_layer((d, d)), of_layer((dp, d))]
    out_specs, out_shape, scratch = [row], [jax.ShapeDtypeStruct((n, d), F32)], []
    per_seq = seq // tm
    if final_g is not None:
        tail = "final"
        args.append(final_g.reshape(1, d))
        specs.append(vec)
    elif next_glu is not None:
        tail = "glu"
        g3, w, bias = next_glu
        args += [g3.reshape(1, d), w[:, :d].astype(BF16), w[:, d:].astype(BF16),
                 bias[:d].reshape(1, d), bias[d:].reshape(1, d)]
        specs += [vec, _resident((d, d)), _resident((d, d)), vec, vec]
        out_specs.append(row)
        out_shape.append(jax.ShapeDtypeStruct((n, d), BF16))
    else:
        tail = "q"
        g3, w_q = next_q
        args += [g3.reshape(1, d), w_q.T.astype(BF16)]
        specs += [vec, _resident((d, d))]
        transposed = lambda rows: pl.BlockSpec((None, rows, tm),
                                               lambda i: (i // per_seq, 0, i % per_seq))
        out_specs.append(transposed(d))
        out_shape.append(jax.ShapeDtypeStruct((n // seq, d, seq), BF16))
        if next_kv is not None:
            tail = "q_kv"
            kv_args, kv_specs = _gate_kv_operands(d, *next_kv)
            args += kv_args
            specs += kv_specs
            out_specs += [transposed(N_HEADS * BF16_ROWS),
                          pl.BlockSpec((None, tm, N_HEADS * LANES),
                                       lambda i: (i // per_seq, i % per_seq, 0)),
                          transposed(d)]
            out_shape += [jax.ShapeDtypeStruct((n // seq, N_HEADS * BF16_ROWS, seq), BF16),
                          jax.ShapeDtypeStruct((n // seq, seq, N_HEADS * LANES), BF16),
                          jax.ShapeDtypeStruct((n // seq, d, seq), BF16)]
            scratch = [pltpu.VMEM((1, LANES), F32)]
    out = pl.pallas_call(
        functools.partial(_ffn_ple_kernel, chunk=1024, with_attn=attn is not None, tail=tail,
                          q_scale=(d // N_HEADS) ** -0.5 * LOG2E, tiles_per_seq=per_seq),
        grid=(n // tm,),
        in_specs=specs,
        out_specs=out_specs,
        out_shape=out_shape,
        scratch_shapes=scratch,
        compiler_params=_params("arbitrary" if tail == "q_kv" else "parallel"),
        name="ffn_ple",
    )(*args)
    return out[0] if tail == "final" else out


def _gate_kv_tile(h, first, g_ref, wf_ref, bf_ref, eq_ref, onesq_ref, wk_ref, wvt_ref, ek_ref,
                  onesk_ref, qa_ref, k_ref, vt_ref, carry_ref):
    @pl.when(first)
    def _():
        carry_ref[...] = jnp.zeros_like(carry_ref)

    tm = h.shape[0]
    xn = _rms(h, g_ref[...]).astype(BF16)
    vt_ref[...] = lax.dot_general(wvt_ref[...], xn, _NT,
                                  preferred_element_type=F32).astype(vt_ref.dtype)
    f = jnp.dot(xn, wf_ref[...], preferred_element_type=F32) + bf_ref[...]
    logf = jnp.minimum(f, 0.0) - jnp.log1p(jnp.exp(-jnp.abs(f)))
    rows = lax.broadcasted_iota(jnp.int32, (tm, tm), 0)
    cols = lax.broadcasted_iota(jnp.int32, (tm, tm), 1)
    tri = jnp.where(rows >= cols, 1.0, 0.0).astype(BF16)
    c = sum(jnp.dot(tri, piece, preferred_element_type=F32) for piece in _split3(logf))
    c = c + carry_ref[...]
    carry_ref[...] = c[tm - 1:tm, :]
    qa = sum(lax.dot_general(eq_ref[r], piece, _NT, preferred_element_type=F32)
             for r, piece in enumerate(_split3(c * LOG2E)))
    qa_ref[...] = (qa + onesq_ref[...]).astype(qa_ref.dtype)
    pieces = jnp.concatenate(_split3(-(c * LOG2E)), axis=1)
    k = jnp.dot(xn, wk_ref[...], preferred_element_type=F32)
    k = k + jnp.dot(pieces, ek_ref[...], preferred_element_type=F32) + onesk_ref[...]
    k_ref[...] = k.astype(k_ref.dtype)


def _gate_kv_operands(d, g, w_k, w_v, w_f, b_f):
    nh = N_HEADS
    dh = d // nh
    heads = jnp.arange(nh)
    wf = jnp.zeros((d, LANES), F32).at[:, :nh].set(w_f).astype(BF16)
    bf = jnp.zeros((1, LANES), F32).at[0, :nh].set(b_f)
    eq = jnp.zeros((3, nh * BF16_ROWS, LANES), F32)
    onesq = jnp.zeros((nh * BF16_ROWS, 1), F32)
    ek = jnp.zeros((3, LANES, nh * LANES), F32)
    onesk = jnp.zeros((1, nh * LANES), F32)
    for r in range(3):
        eq = eq.at[r, heads * BF16_ROWS + r, heads].set(1.0)
        onesq = onesq.at[heads * BF16_ROWS + 3 + r, 0].set(1.0)
        onesk = onesk.at[0, heads * LANES + dh + r].set(1.0)
        ek = ek.at[r, heads, heads * LANES + dh + 3 + r].set(1.0)
    wk = w_k.reshape(d, nh, dh)
    wk = jnp.concatenate([wk, jnp.zeros((d, nh, LANES - dh), wk.dtype)], axis=2)
    args = [g.reshape(1, d), wf, bf, eq.astype(BF16), onesq, wk.reshape(d, nh * LANES).astype(BF16),
            w_v.T.astype(BF16), ek.reshape(3 * LANES, nh * LANES).astype(BF16), onesk]
    specs = [_resident((1, d)), _resident((d, LANES)), _resident((1, LANES)),
             _resident((3, nh * BF16_ROWS, LANES)), _resident((nh * BF16_ROWS, 1)),
             _resident((d, nh * LANES)), _resident((d, d)),
             _resident((3 * LANES, nh * LANES)), _resident((1, nh * LANES))]
    return args, specs


def _attn_kernel(qt_ref, qa_ref, k_ref, vt_ref, o_ref, s_ref, *, dh, nh):
    tq = qt_ref.shape[1]
    tk = s_ref.shape[2]
    nkb = tq // tk
    assert tq == nkb * tk and nkb % 2 == 0
    i = pl.program_id(2)
    kpos = lax.broadcasted_iota(jnp.int32, (tk, tq), 0)
    qpos = lax.broadcasted_iota(jnp.int32, (tk, tq), 1)
    ones = jnp.ones((BF16_ROWS, tk), BF16)
    pad = jnp.zeros((LANES - dh - BF16_ROWS, tq), BF16)

    def scores(hh, j, slot, diag=None):
        kv = pl.ds(pl.multiple_of(j * tk, tk), tk)
        k = k_ref[kv, hh * LANES:(hh + 1) * LANES]
        qt = jnp.concatenate([qt_ref[hh * dh:(hh + 1) * dh, :],
                              qa_ref[hh * BF16_ROWS:(hh + 1) * BF16_ROWS, :], pad], axis=0)
        s = jnp.dot(k, qt, preferred_element_type=F32)
        if diag is not None:
            s = jnp.where(kpos + diag * tk <= qpos, s, NEG_BIG)
        s_ref[slot, hh] = s
        return jnp.max(s, axis=0, keepdims=True)

    def update(hh, j, slot, bmax, carry):
        m, acc = carry
        kv = pl.ds(pl.multiple_of(j * tk, tk), tk)
        vt = jnp.concatenate([vt_ref[hh * dh:(hh + 1) * dh, kv], ones], axis=0)
        m_new = jnp.maximum(m, bmax)
        alpha = jnp.exp2(m - m_new)
        new_acc = []
        for c, acc_c in enumerate(acc):
            cols = slice(c * QCHUNK, (c + 1) * QCHUNK)
            p = jnp.exp2((s_ref[slot, hh, :, cols] - m_new[:, cols]).astype(BF16))
            new_acc.append(alpha[:, cols] * acc_c + jnp.dot(vt, p, preferred_element_type=F32))
        return m_new, tuple(new_acc)

    def half(j, j_next, slot, carry, diag_next=None):
        stats, bmax = carry
        new_stats, new_bmax = [], []
        for hh in range(nh):
            new_bmax.append(scores(hh, j_next, 1 - slot, diag_next))
            new_stats.append(update(hh, j, slot, bmax[hh], stats[hh]))
        return tuple(new_stats), tuple(new_bmax)

    def step(t, carry):
        for d in range(nkb):
            carry = half(nkb * t + d, nkb * t + d + 1, d % 2, carry)
        return carry

    init = (jnp.full((1, tq), NEG_BIG, F32),
            (jnp.zeros((dh + BF16_ROWS, QCHUNK), F32),) * (tq // QCHUNK))
    carry = (init,) * nh, tuple(scores(hh, nkb * i, 0, 0) for hh in range(nh))
    for d in range(nkb - 1):
        carry = half(nkb * i + d, nkb * i + d + 1, d % 2, carry, d + 1)
    carry = half(nkb * i + nkb - 1, 0, 1, carry)
    stats, _ = lax.fori_loop(0, i, step, carry)
    outs = [jnp.concatenate([a[:dh] / a[dh:dh + 1] for a in acc], axis=1) for (_, acc) in stats]
    o_ref[...] = jnp.concatenate(outs, axis=0).T.astype(o_ref.dtype)


def _attention(qt, qa, k_aug, vt):
    b, d, s = vt.shape
    nh = ATTN_HEADS_PER_STEP
    dh = d // N_HEADS
    return pl.pallas_call(
        functools.partial(_attn_kernel, dh=dh, nh=nh),
        grid=(b, N_HEADS // nh, s // TQ),
        in_specs=[pl.BlockSpec((None, nh * dh, TQ), lambda bi, hg, i: (bi, hg, i)),
                  pl.BlockSpec((None, nh * BF16_ROWS, TQ), lambda bi, hg, i: (bi, hg, i)),
                  pl.BlockSpec((None, s, nh * LANES), lambda bi, hg, i: (bi, 0, hg),
                               pipeline_mode=pl.Buffered(1)),
                  pl.BlockSpec((None, nh * dh, s), lambda bi, hg, i: (bi, hg, 0))],
        out_specs=pl.BlockSpec((None, TQ, nh * dh), lambda bi, hg, i: (bi, i, hg)),
        out_shape=jax.ShapeDtypeStruct((b, s, d), BF16),
        scratch_shapes=[pltpu.VMEM((2, nh, TK, TQ), F32)],
        compiler_params=_params("parallel", "parallel", "arbitrary"),
        name="fox_attention",
    )(qt, qa, k_aug, vt)


def kernel(x, p, mix_norm, conv_w_pw1, conv_b_pw1, conv_w_dw, conv_b_dw, conv_ln_g, conv_ln_b,
           conv_w_pw2, conv_b_pw2, kv_norm, w_kvf, b_f, attn_w_q, attn_w_o, ffn_norm, ffn_w1,
           ffn_w2, ple_norm, ple_w_gate, ple_w_proj, final_norm):
    b, s, d = x.shape
    n = b * s
    depth = mix_norm.shape[0]
    n_a = conv_w_pw1.shape[0]
    assert s % TQ == 0 and s % TM == 0 and s % TM_CONV == 0
    assert d == N_HEADS * (d // N_HEADS) and d // N_HEADS + BF16_ROWS <= LANES

    h = x.reshape(n, d)
    p = p.reshape(depth, n, -1)
    ffn_w1, ffn_w2 = ffn_w1.astype(BF16), ffn_w2.astype(BF16)
    ple_w_gate, ple_w_proj = ple_w_gate.astype(BF16), ple_w_proj.astype(BF16)
    assert 1 <= n_a < depth
    qa = k_aug = vt = None
    glu = _pw1_glu(h, mix_norm[0], conv_w_pw1[0], conv_b_pw1[0])
    qt = None
    for i in range(depth):
        attn = None
        if i < n_a:
            h = _conv_block(glu.reshape(b, s, d), h.reshape(b, s, d), conv_w_dw[i], conv_b_dw[i],
                            conv_ln_g[i], conv_ln_b[i], conv_w_pw2[i], conv_b_pw2[i]).reshape(n, d)
        else:
            j = i - n_a
            attn = (_attention(qt, qa, k_aug, vt).reshape(n, d), attn_w_o[j])
        tail = {}
        if i == depth - 1:
            tail["final_g"] = final_norm
        elif i + 1 < n_a:
            tail["next_glu"] = (mix_norm[i + 1], conv_w_pw1[i + 1], conv_b_pw1[i + 1])
        else:
            tail["next_q"] = (mix_norm[i + 1], attn_w_q[i + 1 - n_a])
            if i + 1 == n_a:
                tail["next_kv"] = (kv_norm, w_kvf[:, :d], w_kvf[:, d:2 * d], w_kvf[:, 2 * d:], b_f)
        out = _ffn_ple(h, i, p, ffn_norm[i], ffn_w1, ffn_w2, ple_norm[i], ple_w_gate, ple_w_proj,
                       attn=attn, seq=s, **tail)
        if i == depth - 1:
            h = out
        elif i + 1 < n_a:
            h, glu = out
        elif i + 1 == n_a:
            h, qt, qa, k_aug, vt = out
        else:
            h, qt = out
    return h.reshape(b, s, d)
```
